```python
import math
import jax
import jax.numpy as jnp
from jax import lax
import numpy as np

D_MODEL = 1024
BATCH = 8
SEQ = 8192
DEPTH = 2

GRID_W = 64
N_BRANCHES = 4
BRANCH_W = D_MODEL // N_BRANCHES
HEAD_DIM = 64
N_Q_HEADS = BRANCH_W // HEAD_DIM
N_KV_HEADS = 2
Q_GROUP = N_Q_HEADS // N_KV_HEADS
ROPE_HALF = HEAD_DIM // 2
ROPE_THETA = 10000.0
Q_BLOCK = 128
SC_WIDTH = 3
CF_WIDTH = 31
POOL_WINDOWS = (2, 4, 8, 16)
POOL_GROUP = BRANCH_W // len(POOL_WINDOWS)
N_EXPERTS = 32
TOP_K = 4
D_EXPERT = D_MODEL
SWIGLU_LIMIT = 7.0
SWIGLU_ALPHA = 1.702
MOE_BLOCK = 256
DN_ALPHA = (2.0 * DEPTH) ** 0.25
DN_BETA = (8.0 * DEPTH) ** -0.25
LN_EPS = 1e-5
RMS_EPS = 1e-6

Q_COLS = N_Q_HEADS * HEAD_DIM
KV_COLS = N_KV_HEADS * HEAD_DIM
IN_SIZES = (Q_COLS, KV_COLS, KV_COLS, 3 * BRANCH_W, 2 * BRANCH_W, BRANCH_W, N_BRANCHES * D_MODEL)
IN_COLS = sum(IN_SIZES)
IN_SPLITS = [int(s) for s in np.cumsum(IN_SIZES)[:-1]]

kernel_name = "hybrid_gated_branch_encoder"


def layer_norm(x, g, b):
    xf = x.astype(jnp.float32)
    mu = jnp.mean(xf, axis=-1, keepdims=True)
    var = jnp.mean(jnp.square(xf - mu), axis=-1, keepdims=True)
    return ((xf - mu) * lax.rsqrt(var + LN_EPS) * g + b).astype(x.dtype)


def rms_norm_f32(x, g):
    xf = x.astype(jnp.float32)
    return xf * lax.rsqrt(jnp.mean(jnp.square(xf), axis=-1, keepdims=True) + RMS_EPS) * g


def axial_rope_tables(seq):
    rows = seq // GRID_W
    row = jnp.repeat(jnp.arange(rows, dtype=jnp.float32), GRID_W)
    col = jnp.tile(jnp.arange(GRID_W, dtype=jnp.float32), rows)
    inv = ROPE_THETA ** (-jnp.arange(0, ROPE_HALF, 2, dtype=jnp.float32) / ROPE_HALF)
    ang_r = row[:, None] * inv
    ang_c = col[:, None] * inv
    return (jnp.cos(ang_r), jnp.sin(ang_r), jnp.cos(ang_c), jnp.sin(ang_c))


def apply_axial_rope(x, tabs):
    cr, sr, cc, sc = (t[:, None, :] for t in tabs)
    r1, r2, c1, c2 = jnp.split(x, 4, axis=-1)
    return jnp.concatenate([r1 * cr - r2 * sr, r2 * cr + r1 * sr,
                            c1 * cc - c2 * sc, c2 * cc + c1 * sc], axis=-1)


def gqa_attention(zq, zk, zv, q_gain, k_gain, tabs):
    B, S, _ = zq.shape
    dt = zq.dtype
    q = apply_axial_rope(rms_norm_f32(zq.reshape(B, S, N_Q_HEADS, HEAD_DIM), q_gain), tabs) * (HEAD_DIM ** -0.5)
    k = apply_axial_rope(rms_norm_f32(zk.reshape(B, S, N_KV_HEADS, HEAD_DIM), k_gain), tabs).astype(dt)
    v = zv.reshape(B, S, N_KV_HEADS, HEAD_DIM)
    q = q.astype(dt).reshape(B, S // Q_BLOCK, Q_BLOCK, N_KV_HEADS, Q_GROUP, HEAD_DIM)
    q = jnp.moveaxis(q, 1, 0)

    def block(qb):
        s = jnp.einsum('bqkgd,bskd->bkgqs', qb, k, preferred_element_type=jnp.float32)
        p = jax.nn.softmax(s, axis=-1).astype(dt)
        return jnp.einsum('bkgqs,bskd->bqkgd', p, v)

    o = lax.map(block, q)
    return jnp.moveaxis(o, 0, 1).reshape(B, S, N_Q_HEADS * HEAD_DIM)


def depthwise_conv(x, w):
    K, C = w.shape
    return lax.conv_general_dilated(x, w[:, None, :].astype(x.dtype), window_strides=(1,),
                                    padding=[(K // 2, K // 2)],
                                    dimension_numbers=('NWC', 'WIO', 'NWC'),
                                    feature_group_count=C)


def short_gated_conv(z, conv_w):
    b_gate, c_gate, u = jnp.split(z, 3, axis=-1)
    return b_gate * depthwise_conv(c_gate * u, conv_w)


def conformer_conv(z, conv_w, conv_b, ln_g, ln_b):
    a, g = jnp.split(z, 2, axis=-1)
    y = depthwise_conv(a * jax.nn.sigmoid(g), conv_w) + conv_b
    return jax.nn.silu(layer_norm(y, ln_g, ln_b))


def multiscale_pool(u, pool_w, pool_scale):
    B, S, W = u.shape
    uf = u.astype(jnp.float32)
    csum = jnp.pad(jnp.cumsum(uf, axis=1), ((0, 0), (1, 0), (0, 0)))
    t = jnp.arange(S)
    outs = []
    for i, w in enumerate(POOL_WINDOWS):
        lo = jnp.maximum(t - w // 2, 0)
        hi = jnp.minimum(t + (w - 1 - w // 2), S - 1)
        sl = slice(i * POOL_GROUP, (i + 1) * POOL_GROUP)
        c = csum[:, :, sl]
        mean = (c[:, hi + 1] - c[:, lo]) / (hi - lo + 1).astype(jnp.float32)[:, None]
        outs.append(mean - uf[:, :, sl])
    y = jnp.stack(outs, axis=2).astype(u.dtype)
    y = jnp.einsum('bsgc,gcd->bsgd', y, pool_w).reshape(B, S, W)
    return y * pool_scale


def token_mixer(x, w_in, b_in, q_g, k_g, sc_w, cf_w, cf_b, cf_g, cf_beta,
                pool_w, pool_scale, w_branch, w_out, tabs):
    B, S, D = x.shape
    z = x @ w_in + b_in
    zq, zk, zv, zsc, zcf, zpool, zgate = jnp.split(z, IN_SPLITS, axis=-1)
    branches = jnp.stack([
        gqa_attention(zq, zk, zv, q_g, k_g, tabs),
        short_gated_conv(zsc, sc_w),
        conformer_conv(zcf, cf_w, cf_b, cf_g, cf_beta),
        multiscale_pool(zpool, pool_w, pool_scale),
    ], axis=2)
    proj = jnp.einsum('bsgc,gcd->bsgd', branches, w_branch)
    gates = jax.nn.sigmoid(zgate.reshape(B, S, N_BRANCHES, D))
    merged = jnp.sum(gates * proj, axis=2)
    return merged @ w_out


def moe_ffn(x, w_router, b_router, w_gu, b_gu, w_down, b_down):
    B, S, D = x.shape
    xt = x.reshape(-1, D)
    n_tok = xt.shape[0]
    n_assign = n_tok * TOP_K
    logits = jnp.dot(xt, w_router, preferred_element_type=jnp.float32) + b_router.astype(jnp.float32)
    top_logits, top_idx = lax.top_k(logits, TOP_K)
    top_w = jax.nn.softmax(top_logits, axis=-1)
    flat_e = top_idx.reshape(-1)
    flat_tok = jnp.broadcast_to(jnp.arange(n_tok, dtype=jnp.int32)[:, None], (n_tok, TOP_K)).reshape(-1)
    order = jnp.argsort(flat_e)
    se, stok, sw = flat_e[order], flat_tok[order], top_w.reshape(-1)[order]
    counts = jnp.bincount(flat_e, length=N_EXPERTS)
    start = jnp.cumsum(counts) - counts
    padded = (counts + MOE_BLOCK - 1) // MOE_BLOCK * MOE_BLOCK
    pend = jnp.cumsum(padded)
    pstart = pend - padded
    dest = pstart[se] + jnp.arange(n_assign, dtype=jnp.int32) - start[se]
    n_slots = (n_assign + MOE_BLOCK - 1) // MOE_BLOCK * MOE_BLOCK + N_EXPERTS * MOE_BLOCK
    slot_tok = jnp.zeros((n_slots,), jnp.int32).at[dest].set(stok)
    slot_w = jnp.zeros((n_slots,), jnp.float32).at[dest].set(sw)
    n_blocks = n_slots // MOE_BLOCK
    blk_e = jnp.minimum(jnp.searchsorted(pend, jnp.arange(n_blocks) * MOE_BLOCK, side='right'),
                        N_EXPERTS - 1)

    def expert_block(args):
        tok, e = args
        h = jnp.dot(xt[tok], w_gu[e]) + b_gu[e]
        glu, lin = jnp.split(h, 2, axis=-1)
        glu = jnp.minimum(glu, SWIGLU_LIMIT)
        lin = jnp.clip(lin, -SWIGLU_LIMIT, SWIGLU_LIMIT)
        act = glu * jax.nn.sigmoid(SWIGLU_ALPHA * glu) * (lin + 1.0)
        return jnp.dot(act, w_down[e]) + b_down[e]

    y = lax.map(expert_block, (slot_tok.reshape(n_blocks, MOE_BLOCK), blk_e))
    y = y.reshape(n_slots, D).astype(jnp.float32) * slot_w[:, None]
    out = jax.ops.segment_sum(y, slot_tok, num_segments=n_tok)
    return out.astype(x.dtype).reshape(B, S, D)


def setup_inputs(seed: int = 0) -> dict:
    key = jax.random.key(seed)
    ks = jax.random.split(key, 26)
    f32 = jnp.float32

    def nrm(k, shape, scale):
        return jax.random.normal(k, shape, f32) * scale

    D, W, E, F, L = D_MODEL, BRANCH_W, N_EXPERTS, D_EXPERT, DEPTH
    return {
        "x": nrm(ks[0], (BATCH, SEQ, D), 1.0),
        "ln_in_g": 1.0 + nrm(ks[1], (D,), 0.02),
        "ln_in_b": nrm(ks[2], (D,), 0.02),
        "w_in": nrm(ks[3], (L, D, IN_COLS), D ** -0.5),
        "b_in": nrm(ks[4], (L, IN_COLS), 0.02),
        "q_norm_g": 1.0 + nrm(ks[5], (L, HEAD_DIM), 0.02),
        "k_norm_g": 1.0 + nrm(ks[6], (L, HEAD_DIM), 0.02),
        "sc_conv_w": nrm(ks[7], (L, SC_WIDTH, W), SC_WIDTH ** -0.5),
        "cf_conv_w": nrm(ks[8], (L, CF_WIDTH, W), CF_WIDTH ** -0.5),
        "cf_conv_b": nrm(ks[9], (L, W), 0.02),
        "cf_ln_g": 1.0 + nrm(ks[10], (L, W), 0.02),
        "cf_ln_b": nrm(ks[11], (L, W), 0.02),
        "pool_w": nrm(ks[12], (L, len(POOL_WINDOWS), POOL_GROUP, POOL_GROUP), POOL_GROUP ** -0.5),
        "pool_scale": 1.0 + nrm(ks[13], (L, W), 0.02),
        "w_branch": nrm(ks[14], (L, N_BRANCHES, W, D), DN_BETA * W ** -0.5),
        "w_out": nrm(ks[15], (L, D, D), DN_BETA * D ** -0.5),
        "ln1_g": 1.0 + nrm(ks[16], (L, D), 0.02),
        "ln1_b": nrm(ks[17], (L, D), 0.02),
        "w_router": nrm(ks[18], (L, D, E), D ** -0.5),
        "b_router": nrm(ks[19], (L, E), 0.01),
        "w_gate_up": nrm(ks[20], (L, E, D, 2 * F), D ** -0.5),
        "b_gate_up": nrm(ks[21], (L, E, 2 * F), 0.02),
        "w_down": nrm(ks[22], (L, E, F, D), DN_BETA * F ** -0.5),
        "b_down": nrm(ks[23], (L, E, D), 0.02),
        "ln2_g": 1.0 + nrm(ks[24], (L, D), 0.02),
        "ln2_b": nrm(ks[25], (L, D), 0.02),
    }


def reference(x, ln_in_g, ln_in_b, w_in, b_in, q_norm_g, k_norm_g, sc_conv_w, cf_conv_w,
              cf_conv_b, cf_ln_g, cf_ln_b, pool_w, pool_scale, w_branch, w_out, ln1_g, ln1_b,
              w_router, b_router, w_gate_up, b_gate_up, w_down, b_down, ln2_g, ln2_b):
    tabs = axial_rope_tables(x.shape[1])
    h = layer_norm(x, ln_in_g, ln_in_b)
    for l in range(DEPTH):
        mix = token_mixer(h, w_in[l], b_in[l], q_norm_g[l], k_norm_g[l], sc_conv_w[l],
                          cf_conv_w[l], cf_conv_b[l], cf_ln_g[l], cf_ln_b[l], pool_w[l],
                          pool_scale[l], w_branch[l], w_out[l], tabs)
        h = layer_norm(DN_ALPHA * h + mix, ln1_g[l], ln1_b[l])
        ffn = moe_ffn(h, w_router[l], b_router[l], w_gate_up[l], b_gate_up[l], w_down[l], b_down[l])
        h = layer_norm(DN_ALPHA * h + ffn, ln2_g[l], ln2_b[l])
    return h
```

```python
import functools

import jax
import jax.numpy as jnp
import numpy as np
from jax import lax
from jax.experimental import pallas as pl
from jax.experimental.pallas import tpu as pltpu

F32 = jnp.float32
BF16 = jnp.bfloat16
I32 = jnp.int32

GRID_W = 64
HEAD_DIM = 64
N_Q_HEADS = 4
N_KV_HEADS = 2
Q_GROUP = N_Q_HEADS // N_KV_HEADS
ROPE_HALF = HEAD_DIM // 2
ROPE_THETA = 10000.0
BRANCH_W = 256
N_BRANCHES = 4
SC_WIDTH = 3
CF_WIDTH = 31
POOL_WINDOWS = (2, 4, 8, 16)
POOL_GROUP = BRANCH_W // len(POOL_WINDOWS)
POOL_TAPS = 16
POOL_FIRST = -8
N_EXPERTS = 32
TOP_K = 4
SWIGLU_LIMIT = 7.0
SWIGLU_ALPHA = 1.702
LN_EPS = 1e-5
RMS_EPS = 1e-6
QKV_COLS = N_Q_HEADS * HEAD_DIM + 2 * N_KV_HEADS * HEAD_DIM

LANES = 128
HALO = 16
MOE_BLOCK = 256
VMEM_LIMIT = 56 * 1024 * 1024
NEG_BIG = -1e30


def _cparams(*sem):
    return pltpu.CompilerParams(dimension_semantics=sem, vmem_limit_bytes=VMEM_LIMIT)


def _tile(n, pref):
    t = min(n, pref)
    assert n % t == 0, (n, t)
    return t


def _const_spec(shape):
    nd = len(shape)
    return pl.BlockSpec(shape, lambda *_: (0,) * nd)


def _layer_norm_rows(y, g, b):
    mu = jnp.mean(y, axis=-1, keepdims=True)
    d = y - mu
    var = jnp.mean(d * d, axis=-1, keepdims=True)
    return d * lax.rsqrt(var + LN_EPS) * g + b


def _sigmoid(x):
    return 1.0 / (1.0 + jnp.exp(-x))


def _ln_kernel(x_ref, g_ref, b_ref, h_ref, hb_ref):
    h = _layer_norm_rows(x_ref[...], g_ref[...], b_ref[...])
    h_ref[...] = h
    hb_ref[...] = h.astype(BF16)


def _entry_ln(x2, g, b):
    n, d = x2.shape
    t = _tile(n, 512)
    return pl.pallas_call(
        _ln_kernel,
        grid=(n // t,),
        in_specs=[pl.BlockSpec((t, d), lambda i: (i, 0)), _const_spec((1, d)), _const_spec((1, d))],
        out_specs=[pl.BlockSpec((t, d), lambda i: (i, 0)), pl.BlockSpec((t, d), lambda i: (i, 0))],
        out_shape=[jax.ShapeDtypeStruct((n, d), F32), jax.ShapeDtypeStruct((n, d), BF16)],
        compiler_params=_cparams("parallel"),
        name="entry_ln",
    )(x2, g.reshape(1, d), b.reshape(1, d))


def _qkv_kernel(hb_ref, w_ref, b_ref, gain_ref, scale_ref, cos_ref, sin_ref, q_ref, k_ref, v_ref):
    z = jnp.dot(hb_ref[...], w_ref[...], preferred_element_type=F32) + b_ref[...]
    t = z.shape[0]
    lane = lax.broadcasted_iota(I32, (t, LANES), 1)
    low_head = lane < HEAD_DIM
    first_half = (lane % ROPE_HALF) < (ROPE_HALF // 2)
    cos = cos_ref[...]
    sin = sin_ref[...]
    outs = []
    for c in range(3):
        x = z[:, c * LANES:(c + 1) * LANES]
        x2 = x * x
        s_all = jnp.sum(x2, axis=-1, keepdims=True)
        s_low = jnp.sum(jnp.where(low_head, x2, 0.0), axis=-1, keepdims=True)
        ms = jnp.where(low_head, s_low, s_all - s_low) * (1.0 / HEAD_DIM)
        xn = x * lax.rsqrt(ms + RMS_EPS) * gain_ref[:, c * LANES:(c + 1) * LANES]
        partner = jnp.where(first_half, pltpu.roll(xn, LANES - ROPE_HALF // 2, 1), pltpu.roll(xn, ROPE_HALF // 2, 1))
        y = (xn * cos + partner * sin) * scale_ref[:, c * LANES:(c + 1) * LANES]
        outs.append(y.astype(BF16))
    q_ref[:, 0:LANES] = outs[0]
    q_ref[:, LANES:2 * LANES] = outs[1]
    k_ref[...] = outs[2]
    v_ref[...] = z[:, 3 * LANES:4 * LANES].astype(BF16)


def _qkv_proj(hb, w, b, gain, scale, cos, sin, seq):
    n, d = hb.shape
    t = _tile(seq, 512)
    per_seq = seq // t
    row = lambda i: (i, 0)
    return pl.pallas_call(
        _qkv_kernel,
        grid=(n // t,),
        in_specs=[pl.BlockSpec((t, d), row), _const_spec((d, QKV_COLS)), _const_spec((1, QKV_COLS)),
                  _const_spec((1, 3 * LANES)), _const_spec((1, 3 * LANES)),
                  pl.BlockSpec((t, LANES), lambda i: (i % per_seq, 0)),
                  pl.BlockSpec((t, LANES), lambda i: (i % per_seq, 0))],
        out_specs=[pl.BlockSpec((t, 2 * LANES), row), pl.BlockSpec((t, LANES), row), pl.BlockSpec((t, LANES), row)],
        out_shape=[jax.ShapeDtypeStruct((n, 2 * LANES), BF16), jax.ShapeDtypeStruct((n, LANES), BF16),
                   jax.ShapeDtypeStruct((n, LANES), BF16)],
        compiler_params=_cparams("parallel"),
        name="qkv_proj",
    )(hb, w, b, gain, scale, cos, sin)


def _zproj_kernel(hb_ref, wsc_ref, wcf_ref, wpl_ref, wg_ref, bsc_ref, bcf_ref, bpl_ref, bg_ref,
                  zsc_ref, zcf_ref, zpl_ref, zg_ref):
    x = hb_ref[...]
    for w_ref, b_ref, o_ref in ((wsc_ref, bsc_ref, zsc_ref), (wcf_ref, bcf_ref, zcf_ref),
                                (wpl_ref, bpl_ref, zpl_ref), (wg_ref, bg_ref, zg_ref)):
        o_ref[...] = (jnp.dot(x, w_ref[...], preferred_element_type=F32) + b_ref[...]).astype(BF16)


def _z_proj(hb, ws, bs):
    n, d = hb.shape
    t = _tile(n, 256)
    row = lambda i: (i, 0)
    widths = [w.shape[1] for w in ws]
    return pl.pallas_call(
        _zproj_kernel,
        grid=(n // t,),
        in_specs=[pl.BlockSpec((t, d), row)]
        + [pl.BlockSpec((d, c), lambda i: (0, 0), pipeline_mode=pl.Buffered(1)) for c in widths]
        + [_const_spec((1, c)) for c in widths],
        out_specs=[pl.BlockSpec((t, c), row) for c in widths],
        out_shape=[jax.ShapeDtypeStruct((n, c), BF16) for c in widths],
        compiler_params=_cparams("parallel"),
        name="z_proj",
    )(hb, *ws, *bs)


def _attn_kernel(q_ref, kt_ref, va_ref, vb_ref, o_ref, m_scr, l_scr, acc_scr):
    g, tq, hd = q_ref.shape
    rows = g * tq
    n_chunks = kt_ref.shape[0]
    q = q_ref[...].reshape(rows, hd)
    m_scr[...] = jnp.full(m_scr.shape, -jnp.inf, F32)
    l_scr[...] = jnp.zeros(l_scr.shape, F32)
    acc_scr[...] = jnp.zeros(acc_scr.shape, F32)

    def body(c, carry):
        s = jnp.dot(q, kt_ref[c], preferred_element_type=F32)
        m_prev = m_scr[...]
        m_new = jnp.maximum(m_prev, jnp.max(s, axis=-1, keepdims=True))
        alpha = jnp.exp(m_prev - m_new)
        p = jnp.exp(s - m_new)
        l_scr[...] = alpha * l_scr[...] + jnp.sum(p, axis=-1, keepdims=True)
        pb = p.astype(BF16)
        pv_a = jnp.dot(pb[:tq], va_ref[c], preferred_element_type=F32)
        pv_b = jnp.dot(pb[tq:], vb_ref[c], preferred_element_type=F32)
        acc_scr[0:tq, :] = alpha[:tq] * acc_scr[0:tq, :] + pv_a
        acc_scr[tq:rows, :] = alpha[tq:] * acc_scr[tq:rows, :] + pv_b
        m_scr[...] = m_new
        return carry

    lax.fori_loop(0, n_chunks, body, 0)
    res = acc_scr[...] / l_scr[...]
    o_ref[...] = (res[:tq] + res[tq:]).astype(BF16)


def _attention(q, k, v, batch, seq):
    tq = _tile(seq, 256)
    tk = _tile(seq, 512)
    nc = seq // tk
    q5 = q.reshape(batch, seq, N_KV_HEADS, Q_GROUP, HEAD_DIM).transpose(0, 2, 3, 1, 4)
    kt = k.reshape(batch, nc, tk, N_KV_HEADS, HEAD_DIM).transpose(0, 3, 1, 4, 2)
    v5 = v.reshape(batch, nc, tk, N_KV_HEADS, HEAD_DIM).transpose(0, 3, 1, 2, 4)
    zeros = jnp.zeros_like(v5)
    va = jnp.concatenate([v5, zeros], axis=-1)
    vb = jnp.concatenate([zeros, v5], axis=-1)
    rows = Q_GROUP * tq
    return pl.pallas_call(
        _attn_kernel,
        grid=(batch, N_KV_HEADS, seq // tq),
        in_specs=[pl.BlockSpec((None, None, Q_GROUP, tq, HEAD_DIM), lambda b, h, i: (b, h, 0, i, 0)),
                  pl.BlockSpec((None, None, nc, HEAD_DIM, tk), lambda b, h, i: (b, h, 0, 0, 0)),
                  pl.BlockSpec((None, None, nc, tk, LANES), lambda b, h, i: (b, h, 0, 0, 0)),
                  pl.BlockSpec((None, None, nc, tk, LANES), lambda b, h, i: (b, h, 0, 0, 0))],
        out_specs=pl.BlockSpec((None, tq, LANES), lambda b, h, i: (b, i, h)),
        out_shape=jax.ShapeDtypeStruct((batch, seq, N_KV_HEADS * LANES), BF16),
        scratch_shapes=[pltpu.VMEM((rows, 1), F32), pltpu.VMEM((rows, 1), F32), pltpu.VMEM((rows, LANES), F32)],
        compiler_params=_cparams("parallel", "parallel", "arbitrary"),
        name="attention",
    )(q5, kt, va, vb).reshape(batch * seq, N_KV_HEADS * LANES)


def _dwconv(xs_ref, w_ref, n_taps, first_row, rows):
    acc = None
    for j in range(n_taps):
        term = xs_ref[pl.ds(first_row + j, rows), :] * w_ref[j:j + 1, :]
        acc = term if acc is None else acc + term
    return acc


def _mixer_kernel(seq, alpha,
                  o_ref, scp_ref, scc_ref, scn_ref, cfp_ref, cfc_ref, cfn_ref, plp_ref, plc_ref, pln_ref,
                  zg_ref, h_ref, scw_ref, cfw_ref, cfb_ref, cfg_ref, cfbeta_ref, plmask_ref, pllo_ref, plhi_ref,
                  plw_ref, pls_ref, wbr_ref, wout_ref, ln1g_ref, ln1b_ref, wr_ref, br_ref,
                  h1_ref, ridx_ref, rw_ref, rrank_ref, cnt_ref,
                  xsc, xcf, xpl, cnt_scr):
    t = h_ref.shape[0]
    w = BRANCH_W
    i = pl.program_id(0)
    per_seq = seq // t
    j = i % per_seq
    has_prev = (j > 0).astype(F32)
    has_next = (j < per_seq - 1).astype(F32)

    def sc_v(z):
        z = z.astype(F32)
        return z[:, w:2 * w] * z[:, 2 * w:3 * w]
    xsc[0:HALO, :] = sc_v(scp_ref[...]) * has_prev
    xsc[HALO:HALO + t, :] = sc_v(scc_ref[...])
    xsc[HALO + t:2 * HALO + t, :] = sc_v(scn_ref[...]) * has_next
    y_sc = scc_ref[:, 0:w].astype(F32) * _dwconv(xsc, scw_ref, SC_WIDTH, HALO - SC_WIDTH // 2, t)

    def cf_v(z):
        z = z.astype(F32)
        return z[:, 0:w] * _sigmoid(z[:, w:2 * w])
    xcf[0:HALO, :] = cf_v(cfp_ref[...]) * has_prev
    xcf[HALO:HALO + t, :] = cf_v(cfc_ref[...])
    xcf[HALO + t:2 * HALO + t, :] = cf_v(cfn_ref[...]) * has_next
    y_cf = _dwconv(xcf, cfw_ref, CF_WIDTH, HALO - CF_WIDTH // 2, t) + cfb_ref[...]
    y_cf = _layer_norm_rows(y_cf, cfg_ref[...], cfbeta_ref[...])
    y_cf = y_cf * _sigmoid(y_cf)

    u = plc_ref[...].astype(F32)
    xpl[0:HALO, :] = plp_ref[...].astype(F32) * has_prev
    xpl[HALO:HALO + t, :] = u
    xpl[HALO + t:2 * HALO + t, :] = pln_ref[...].astype(F32) * has_next
    win_sum = _dwconv(xpl, plmask_ref, POOL_TAPS, HALO + POOL_FIRST, t)
    pos = (j * t + lax.broadcasted_iota(I32, (t, w), 0)).astype(F32)
    n_win = jnp.minimum(pos + plhi_ref[...], float(seq - 1)) - jnp.maximum(pos - pllo_ref[...], 0.0) + 1.0
    y_pl = win_sum / n_win - u
    y_pl = jnp.dot(y_pl.astype(BF16), plw_ref[...], preferred_element_type=F32) * pls_ref[...]

    merged = None
    for g, br in enumerate((o_ref[...], y_sc.astype(BF16), y_cf.astype(BF16), y_pl.astype(BF16))):
        proj = jnp.dot(br, wbr_ref[g], preferred_element_type=F32)
        gate = _sigmoid(zg_ref[:, g * proj.shape[1]:(g + 1) * proj.shape[1]].astype(F32))
        merged = gate * proj if merged is None else merged + gate * proj
    mix = jnp.dot(merged.astype(BF16), wout_ref[...], preferred_element_type=F32)
    h1 = _layer_norm_rows(alpha * h_ref[...] + mix, ln1g_ref[...], ln1b_ref[...])
    h1_ref[...] = h1

    logits = jnp.dot(h1.astype(BF16), wr_ref[...], preferred_element_type=F32) + br_ref[...]
    lane = lax.broadcasted_iota(I32, (t, LANES), 1)
    vals = logits
    tops, sels, idxs = [], [], []
    for _ in range(TOP_K):
        m = jnp.max(vals, axis=-1, keepdims=True)
        first = jnp.min(jnp.where(vals == m, lane, LANES), axis=-1, keepdims=True)
        sel = lane == first
        vals = jnp.where(sel, -jnp.inf, vals)
        tops.append(m)
        sels.append(sel)
        idxs.append(first)
    exps = [jnp.exp(m - tops[0]) for m in tops]
    denom = exps[0] + exps[1] + exps[2] + exps[3]

    @pl.when(i == 0)
    def _():
        cnt_scr[...] = jnp.zeros(cnt_scr.shape, F32)

    onehot = jnp.where(sels[0] | sels[1] | sels[2] | sels[3], 1.0, 0.0)
    r_id = lax.broadcasted_iota(I32, (t, t), 0)
    c_id = lax.broadcasted_iota(I32, (t, t), 1)
    lower = jnp.where(c_id < r_id, 1.0, 0.0).astype(BF16)
    rank_e = jnp.dot(lower, onehot.astype(BF16), preferred_element_type=F32) + cnt_scr[...]
    new_cnt = cnt_scr[...] + jnp.sum(onehot, axis=0, keepdims=True)
    cnt_scr[...] = new_cnt
    cnt_ref[...] = new_cnt.astype(I32)

    ridx = jnp.zeros((t, LANES), I32)
    rw = jnp.zeros((t, LANES), F32)
    rrank = jnp.zeros((t, LANES), I32)
    for k in range(TOP_K):
        rk = jnp.sum(jnp.where(sels[k], rank_e, 0.0), axis=-1, keepdims=True)
        ridx = jnp.where(lane == k, idxs[k], ridx)
        rw = jnp.where(lane == k, exps[k] / denom, rw)
        rrank = jnp.where(lane == k, rk.astype(I32), rrank)
    ridx_ref[...] = ridx
    rw_ref[...] = rw
    rrank_ref[...] = rrank


def _mixer(o, zsc, zcf, zpl, zg, h, p, seq, alpha):
    n, d = h.shape
    t = _tile(seq, 256)
    hb = t // HALO
    n_halo = n // HALO
    row = lambda i: (i, 0)
    prev = lambda i: (jnp.maximum(i * hb - 1, 0), 0)
    nxt = lambda i: (jnp.minimum((i + 1) * hb, n_halo - 1), 0)

    def halo_specs(c):
        return [pl.BlockSpec((HALO, c), prev), pl.BlockSpec((t, c), row), pl.BlockSpec((HALO, c), nxt)]

    consts = [p["sc_w"], p["cf_w"], p["cf_b"], p["cf_g"], p["cf_beta"], p["pl_mask"], p["pl_lo"], p["pl_hi"],
              p["pl_w"], p["pl_s"], p["w_br"], p["w_out"], p["ln1_g"], p["ln1_b"], p["w_r"], p["b_r"]]
    lane_out = lambda dt: jax.ShapeDtypeStruct((n, LANES), dt)
    return pl.pallas_call(
        functools.partial(_mixer_kernel, seq, alpha),
        grid=(n // t,),
        in_specs=[pl.BlockSpec((t, BRANCH_W), row)] + halo_specs(3 * BRANCH_W) + halo_specs(2 * BRANCH_W)
        + halo_specs(BRANCH_W) + [pl.BlockSpec((t, N_BRANCHES * d), row), pl.BlockSpec((t, d), row)]
        + [_const_spec(c.shape) for c in consts],
        out_specs=[pl.BlockSpec((t, d), row), pl.BlockSpec((t, LANES), row), pl.BlockSpec((t, LANES), row),
                   pl.BlockSpec((t, LANES), row), _const_spec((1, LANES))],
        out_shape=[jax.ShapeDtypeStruct((n, d), F32), lane_out(I32), lane_out(F32), lane_out(I32),
                   jax.ShapeDtypeStruct((1, LANES), I32)],
        scratch_shapes=[pltpu.VMEM((t + 2 * HALO, BRANCH_W), F32)] * 3 + [pltpu.VMEM((1, LANES), F32)],
        compiler_params=_cparams("arbitrary"),
        name="mixer",
    )(o, zsc, zsc, zsc, zcf, zcf, zcf, zpl, zpl, zpl, zg, h, *consts)


def _row_copy(src, src_row, dst, dst_row, sem):
    return pltpu.make_async_copy(src.at[pl.ds(src_row, 1), :], dst.at[pl.ds(dst_row, 1), :], sem)


def _dispatch_kernel(tile, pad_start_ref, pad_cnt_ref, n_used_ref, dest_hbm, x_hbm, xs_hbm, idx_smem, zrow, idx_sem, row_sem, pad_sem):
    i = pl.program_id(0)
    n_steps = pl.num_programs(0)
    n_idx = tile * TOP_K

    def idx_copy(step, slot):
        return pltpu.make_async_copy(dest_hbm.at[pl.ds(step * n_idx, n_idx)], idx_smem.at[slot], idx_sem.at[slot])

    def wait_rows():
        pltpu.make_async_copy(x_hbm.at[pl.ds(0, n_idx), :], xs_hbm.at[pl.ds(0, n_idx), :], row_sem).wait()

    @pl.when(i == 0)
    def _():
        idx_copy(0, 0).start()
        zrow[...] = jnp.zeros(zrow.shape, F32)
        n_blocks = xs_hbm.shape[0] // MOE_BLOCK

        def block_copy(b):
            return pltpu.make_async_copy(zrow, xs_hbm.at[pl.ds(b * MOE_BLOCK, MOE_BLOCK), :], pad_sem)

        def fill_block(b, c):
            block_copy(b).start()
            return c
        lax.fori_loop(n_used_ref[0], n_blocks, fill_block, 0)
        for e in range(N_EXPERTS):
            def fill(r, c, e=e):
                _row_copy(zrow, 0, xs_hbm, pad_start_ref[e] + r, pad_sem).start()
                return c
            lax.fori_loop(0, pad_cnt_ref[e], fill, 0)
        for e in range(N_EXPERTS):
            def drain(r, c, e=e):
                _row_copy(zrow, 0, xs_hbm, pad_start_ref[e] + r, pad_sem).wait()
                return c
            lax.fori_loop(0, pad_cnt_ref[e], drain, 0)

        def drain_block(b, c):
            block_copy(b).wait()
            return c
        lax.fori_loop(n_used_ref[0], n_blocks, drain_block, 0)

    slot = i % 2
    idx_copy(i, slot).wait()

    @pl.when(i + 1 < n_steps)
    def _():
        idx_copy(i + 1, 1 - slot).start()

    @pl.when(i > 0)
    def _():
        wait_rows()

    def issue(tok, c):
        for k in range(TOP_K):
            _row_copy(x_hbm, i * tile + tok, xs_hbm, idx_smem[slot, tok * TOP_K + k], row_sem).start()
        return c
    lax.fori_loop(0, tile, issue, 0, unroll=8)

    @pl.when(i == n_steps - 1)
    def _():
        wait_rows()


def _dispatch(x, dest_flat, pad_start, pad_cnt, n_used, n_slots):
    n, d = x.shape
    tile = _tile(n, 512)
    return pl.pallas_call(
        functools.partial(_dispatch_kernel, tile),
        grid_spec=pltpu.PrefetchScalarGridSpec(
            num_scalar_prefetch=3,
            grid=(n // tile,),
            in_specs=[pl.BlockSpec(memory_space=pl.ANY), pl.BlockSpec(memory_space=pl.ANY)],
            out_specs=pl.BlockSpec(memory_space=pl.ANY),
            scratch_shapes=[pltpu.SMEM((2, tile * TOP_K), I32), pltpu.VMEM((MOE_BLOCK, d), F32),
                            pltpu.SemaphoreType.DMA((2,)), pltpu.SemaphoreType.DMA, pltpu.SemaphoreType.DMA],
        ),
        out_shape=jax.ShapeDtypeStruct((n_slots, d), F32),
        compiler_params=pltpu.CompilerParams(dimension_semantics=("arbitrary",), has_side_effects=True),
        name="moe_dispatch",
    )(pad_start, pad_cnt, n_used, dest_flat, x)


def _expert_kernel(blk_e_ref, n_used_ref, xs_ref, wgu_ref, bgu_ref, wd_ref, bd_ref, y_ref):
    b = pl.program_id(0)

    @pl.when(b < n_used_ref[0])
    def _():
        f = wd_ref.shape[0]
        x = xs_ref[...].astype(BF16)
        hgu = jnp.dot(x, wgu_ref[...], preferred_element_type=F32) + bgu_ref[...]
        glu = jnp.minimum(hgu[:, :f], SWIGLU_LIMIT)
        lin = jnp.clip(hgu[:, f:], -SWIGLU_LIMIT, SWIGLU_LIMIT)
        act = glu * _sigmoid(SWIGLU_ALPHA * glu) * (lin + 1.0)
        y_ref[...] = jnp.dot(act.astype(BF16), wd_ref[...], preferred_element_type=F32) + bd_ref[...]

    @pl.when(b >= n_used_ref[0])
    def _():
        y_ref[...] = jnp.zeros(y_ref.shape, F32)


def _experts(xs, blk_e, n_used, wgu, bgu, wd, bd):
    n_slots, d = xs.shape
    nb = n_slots // MOE_BLOCK
    f2 = wgu.shape[2]
    last = lambda b, be, nu: (jnp.minimum(b, nu[0] - 1), 0)
    return pl.pallas_call(
        _expert_kernel,
        grid_spec=pltpu.PrefetchScalarGridSpec(
            num_scalar_prefetch=2,
            grid=(nb,),
            in_specs=[pl.BlockSpec((MOE_BLOCK, d), last),
                      pl.BlockSpec((None, d, f2), lambda b, be, nu: (be[b], 0, 0)),
                      pl.BlockSpec((None, 1, f2), lambda b, be, nu: (be[b], 0, 0)),
                      pl.BlockSpec((None, f2 // 2, d), lambda b, be, nu: (be[b], 0, 0)),
                      pl.BlockSpec((None, 1, d), lambda b, be, nu: (be[b], 0, 0))],
            out_specs=pl.BlockSpec((MOE_BLOCK, d), lambda b, be, nu: (b, 0)),
        ),
        out_shape=jax.ShapeDtypeStruct((n_slots, d), F32),
        compiler_params=_cparams("arbitrary"),
        name="moe_experts",
    )(blk_e, n_used, xs, wgu, bgu, wd, bd)


def _combine_kernel(tile, alpha, dest_hbm, y_hbm, rw_ref, h1_ref, g_ref, b_ref, h_ref, hb_ref,
                    idx_smem, buf, idx_sem, row_sem):
    i = pl.program_id(0)
    n_steps = pl.num_programs(0)
    n_idx = tile * TOP_K

    def idx_copy(step, slot):
        return pltpu.make_async_copy(dest_hbm.at[pl.ds(step * n_idx, n_idx)], idx_smem.at[slot], idx_sem.at[slot])

    def issue_rows(slot):
        def issue(tok, c):
            for k in range(TOP_K):
                pltpu.make_async_copy(y_hbm.at[pl.ds(idx_smem[slot, tok * TOP_K + k], 1), :],
                                      buf.at[slot, k, pl.ds(tok, 1), :], row_sem.at[slot]).start()
            return c
        lax.fori_loop(0, tile, issue, 0, unroll=8)

    slot = i % 2

    @pl.when(i == 0)
    def _():
        idx_copy(0, 0).start()
        idx_copy(0, 0).wait()
        issue_rows(0)

        @pl.when(n_steps > 1)
        def _():
            idx_copy(1, 1).start()

    @pl.when(i + 1 < n_steps)
    def _():
        idx_copy(i + 1, 1 - slot).wait()
        issue_rows(1 - slot)

    @pl.when(i + 2 < n_steps)
    def _():
        idx_copy(i + 2, slot).start()

    pltpu.make_async_copy(buf.at[slot], buf.at[slot], row_sem.at[slot]).wait()

    lane = lax.broadcasted_iota(I32, rw_ref.shape, 1)
    rw = rw_ref[...]
    ffn = None
    for k in range(TOP_K):
        wk = jnp.sum(jnp.where(lane == k, rw, 0.0), axis=-1, keepdims=True)
        term = wk * buf[slot, k]
        ffn = term if ffn is None else ffn + term
    h = _layer_norm_rows(alpha * h1_ref[...] + ffn, g_ref[...], b_ref[...])
    h_ref[...] = h
    hb_ref[...] = h.astype(BF16)


def _combine(y, dest_flat, rw, h1, g, b, alpha):
    n, d = h1.shape
    tile = _tile(n, 256)
    row = lambda i: (i, 0)
    return pl.pallas_call(
        functools.partial(_combine_kernel, tile, alpha),
        grid=(n // tile,),
        in_specs=[pl.BlockSpec(memory_space=pl.ANY), pl.BlockSpec(memory_space=pl.ANY),
                  pl.BlockSpec((tile, LANES), row), pl.BlockSpec((tile, d), row), _const_spec((1, d)), _const_spec((1, d))],
        out_specs=[pl.BlockSpec((tile, d), row), pl.BlockSpec((tile, d), row)],
        out_shape=[jax.ShapeDtypeStruct((n, d), F32), jax.ShapeDtypeStruct((n, d), BF16)],
        scratch_shapes=[pltpu.SMEM((2, tile * TOP_K), I32), pltpu.VMEM((2, TOP_K, tile, d), F32),
                        pltpu.SemaphoreType.DMA((2,)), pltpu.SemaphoreType.DMA((2,))],
        compiler_params=_cparams("arbitrary"),
        name="moe_combine",
    )(dest_flat, y, rw, h1, g.reshape(1, d), b.reshape(1, d))


def _rope_tables(seq):
    rows = seq // GRID_W
    row = jnp.repeat(jnp.arange(rows, dtype=F32), GRID_W)
    col = jnp.tile(jnp.arange(GRID_W, dtype=F32), rows)
    inv = ROPE_THETA ** (-jnp.arange(0, ROPE_HALF, 2, dtype=F32) / ROPE_HALF)
    ang_r = row[:, None] * inv
    ang_c = col[:, None] * inv
    cr, sr, cc, sc = jnp.cos(ang_r), jnp.sin(ang_r), jnp.cos(ang_c), jnp.sin(ang_c)
    cos = jnp.concatenate([cr, cr, cc, cc], axis=-1)
    sin = jnp.concatenate([-sr, sr, -sc, sc], axis=-1)
    return jnp.tile(cos, (1, 2)), jnp.tile(sin, (1, 2))


def _pool_constants():
    offs = np.arange(POOL_TAPS) + POOL_FIRST
    mask = np.zeros((POOL_TAPS, BRANCH_W), np.float32)
    lo = np.zeros((1, BRANCH_W), np.float32)
    hi = np.zeros((1, BRANCH_W), np.float32)
    for g, win in enumerate(POOL_WINDOWS):
        sl = slice(g * POOL_GROUP, (g + 1) * POOL_GROUP)
        mask[(offs >= -(win // 2)) & (offs <= win - 1 - win // 2), sl] = 1.0
        lo[0, sl] = win // 2
        hi[0, sl] = win - 1 - win // 2
    return jnp.asarray(mask), jnp.asarray(lo), jnp.asarray(hi)


def _block_diag(blocks):
    g, a, b = blocks.shape
    out = jnp.zeros((g * a, g * b), blocks.dtype)
    for i in range(g):
        out = out.at[i * a:(i + 1) * a, i * b:(i + 1) * b].set(blocks[i])
    return out


def kernel(x, ln_in_g, ln_in_b, w_in, b_in, q_norm_g, k_norm_g, sc_conv_w, cf_conv_w, cf_conv_b, cf_ln_g, cf_ln_b, pool_w, pool_scale, w_branch, w_out, ln1_g, ln1_b, w_router, b_router, w_gate_up, b_gate_up, w_down, b_down, ln2_g, ln2_b):
    batch, seq, d = x.shape
    depth = w_in.shape[0]
    n = batch * seq
    alpha = float((2.0 * depth) ** 0.25)
    n_assign = n * TOP_K
    n_slots = (n_assign + MOE_BLOCK - 1) // MOE_BLOCK * MOE_BLOCK + N_EXPERTS * MOE_BLOCK
    n_blocks = n_slots // MOE_BLOCK

    cos, sin = _rope_tables(seq)
    pl_mask, pl_lo, pl_hi = _pool_constants()
    q_scale = jnp.concatenate([jnp.full((1, 2 * LANES), HEAD_DIM ** -0.5, F32), jnp.ones((1, LANES), F32)], axis=-1)
    splits = np.cumsum([QKV_COLS, 3 * BRANCH_W, 2 * BRANCH_W, BRANCH_W])

    h, hb = _entry_ln(x.reshape(n, d), ln_in_g, ln_in_b)
    for l in range(depth):
        wl = w_in[l].astype(BF16)
        bl = b_in[l].reshape(1, -1)
        w_parts = [wl[:, :splits[0]]] + [wl[:, splits[i]:splits[i + 1]] for i in range(3)] + [wl[:, splits[3]:]]
        b_parts = [bl[:, :splits[0]]] + [bl[:, splits[i]:splits[i + 1]] for i in range(3)] + [bl[:, splits[3]:]]
        gain = jnp.concatenate([jnp.tile(q_norm_g[l], N_Q_HEADS), jnp.tile(k_norm_g[l], N_KV_HEADS)]).reshape(1, -1)

        q, k, v = _qkv_proj(hb, w_parts[0], b_parts[0], gain, q_scale, cos, sin, seq)
        zsc, zcf, zpl, zg = _z_proj(hb, w_parts[1:], b_parts[1:])
        o = _attention(q, k, v, batch, seq)

        r_pad = LANES - N_EXPERTS
        params = dict(
            sc_w=sc_conv_w[l], cf_w=cf_conv_w[l], cf_b=cf_conv_b[l].reshape(1, -1), cf_g=cf_ln_g[l].reshape(1, -1),
            cf_beta=cf_ln_b[l].reshape(1, -1), pl_mask=pl_mask, pl_lo=pl_lo, pl_hi=pl_hi,
            pl_w=_block_diag(pool_w[l]).astype(BF16), pl_s=pool_scale[l].reshape(1, -1),
            w_br=w_branch[l].astype(BF16), w_out=w_out[l].astype(BF16),
            ln1_g=ln1_g[l].reshape(1, -1), ln1_b=ln1_b[l].reshape(1, -1),
            w_r=jnp.pad(w_router[l], ((0, 0), (0, r_pad))).astype(BF16),
            b_r=jnp.pad(b_router[l].astype(F32), (0, r_pad), constant_values=NEG_BIG).reshape(1, -1),
        )
        h1, ridx, rw, rrank, cnt = _mixer(o, zsc, zcf, zpl, zg, h, params, seq, alpha)

        counts = cnt[0, :N_EXPERTS]
        padded = (counts + MOE_BLOCK - 1) // MOE_BLOCK * MOE_BLOCK
        pend = jnp.cumsum(padded)
        pstart = pend - padded
        dest = (pstart[ridx[:, :TOP_K]] + rrank[:, :TOP_K]).reshape(-1).astype(I32)
        blk_e = jnp.minimum(jnp.searchsorted(pend, jnp.arange(n_blocks, dtype=I32) * MOE_BLOCK, side='right'),
                            N_EXPERTS - 1).astype(I32)
        n_used = (pend[-1:] // MOE_BLOCK).astype(I32)

        xs = _dispatch(h1, dest, (pstart + counts).astype(I32), (padded - counts).astype(I32), n_used, n_slots)
        y = _experts(xs, blk_e, n_used, w_gate_up[l].astype(BF16), b_gate_up[l][:, None, :],
                     w_down[l].astype(BF16), b_down[l][:, None, :])
        h, hb = _combine(y, dest, rw, h1, ln2_g[l], ln2_b[l], alpha)
    return h.reshape(batch, seq, d)
```

```python
import functools

import jax
import jax.numpy as jnp
import numpy as np
from jax import lax
from jax.experimental import pallas as pl
from jax.experimental.pallas import tpu as pltpu

F32 = jnp.float32
BF16 = jnp.bfloat16
I32 = jnp.int32

GRID_W = 64
HEAD_DIM = 64
N_Q_HEADS = 4
N_KV_HEADS = 2
Q_GROUP = N_Q_HEADS // N_KV_HEADS
ROPE_HALF = HEAD_DIM // 2
ROPE_THETA = 10000.0
BRANCH_W = 256
N_BRANCHES = 4
SC_WIDTH = 3
CF_WIDTH = 31
POOL_WINDOWS = (2, 4, 8, 16)
POOL_GROUP = BRANCH_W // len(POOL_WINDOWS)
POOL_TAPS = 16
POOL_FIRST = -8
N_EXPERTS = 32
TOP_K = 4
SWIGLU_LIMIT = 7.0
SWIGLU_ALPHA = 1.702
LN_EPS = 1e-5
RMS_EPS = 1e-6
QKV_COLS = N_Q_HEADS * HEAD_DIM + 2 * N_KV_HEADS * HEAD_DIM

LANES = 128
HALO = 16
MOE_BLOCK = 256
VMEM_LIMIT = 56 * 1024 * 1024
NEG_BIG = -1e30


def _cparams(*sem):
    return pltpu.CompilerParams(dimension_semantics=sem, vmem_limit_bytes=VMEM_LIMIT)


def _tile(n, pref):
    t = min(n, pref)
    assert n % t == 0, (n, t)
    return t


def _const_spec(shape):
    nd = len(shape)
    return pl.BlockSpec(shape, lambda *_: (0,) * nd)


def _layer_norm_rows(y, g, b):
    mu = jnp.mean(y, axis=-1, keepdims=True)
    d = y - mu
    var = jnp.mean(d * d, axis=-1, keepdims=True)
    return d * lax.rsqrt(var + LN_EPS) * g + b


def _sigmoid(x):
    return 1.0 / (1.0 + jnp.exp(-x))


def _ln_kernel(x_ref, g_ref, b_ref, h_ref, hb_ref):
    h = _layer_norm_rows(x_ref[...], g_ref[...], b_ref[...])
    h_ref[...] = h
    hb_ref[...] = h.astype(BF16)


def _entry_ln(x2, g, b):
    n, d = x2.shape
    t = _tile(n, 512)
    return pl.pallas_call(
        _ln_kernel,
        grid=(n // t,),
        in_specs=[pl.BlockSpec((t, d), lambda i: (i, 0)), _const_spec((1, d)), _const_spec((1, d))],
        out_specs=[pl.BlockSpec((t, d), lambda i: (i, 0)), pl.BlockSpec((t, d), lambda i: (i, 0))],
        out_shape=[jax.ShapeDtypeStruct((n, d), F32), jax.ShapeDtypeStruct((n, d), BF16)],
        compiler_params=_cparams("parallel"),
        name="entry_ln",
    )(x2, g.reshape(1, d), b.reshape(1, d))


def _qkv_kernel(hb_ref, w_ref, b_ref, gain_ref, scale_ref, cos_ref, sin_ref, q_ref, k_ref, v_ref):
    z = jnp.dot(hb_ref[...], w_ref[...], preferred_element_type=F32) + b_ref[...]
    t = z.shape[0]
    lane = lax.broadcasted_iota(I32, (t, LANES), 1)
    low_head = lane < HEAD_DIM
    first_half = (lane % ROPE_HALF) < (ROPE_HALF // 2)
    cos = cos_ref[...]
    sin = sin_ref[...]
    outs = []
    for c in range(3):
        x = z[:, c * LANES:(c + 1) * LANES]
        x2 = x * x
        s_all = jnp.sum(x2, axis=-1, keepdims=True)
        s_low = jnp.sum(jnp.where(low_head, x2, 0.0), axis=-1, keepdims=True)
        ms = jnp.where(low_head, s_low, s_all - s_low) * (1.0 / HEAD_DIM)
        xn = x * lax.rsqrt(ms + RMS_EPS) * gain_ref[:, c * LANES:(c + 1) * LANES]
        partner = jnp.where(first_half, pltpu.roll(xn, LANES - ROPE_HALF // 2, 1), pltpu.roll(xn, ROPE_HALF // 2, 1))
        y = (xn * cos + partner * sin) * scale_ref[:, c * LANES:(c + 1) * LANES]
        outs.append(y.astype(BF16))
    q_ref[:, 0:LANES] = outs[0]
    q_ref[:, LANES:2 * LANES] = outs[1]
    k_ref[...] = outs[2]
    v_ref[...] = z[:, 3 * LANES:4 * LANES].astype(BF16)


def _qkv_proj(hb, w, b, gain, scale, cos, sin, seq):
    n, d = hb.shape
    t = _tile(seq, 512)
    per_seq = seq // t
    row = lambda i: (i, 0)
    return pl.pallas_call(
        _qkv_kernel,
        grid=(n // t,),
        in_specs=[pl.BlockSpec((t, d), row), _const_spec((d, QKV_COLS)), _const_spec((1, QKV_COLS)),
                  _const_spec((1, 3 * LANES)), _const_spec((1, 3 * LANES)),
                  pl.BlockSpec((t, LANES), lambda i: (i % per_seq, 0)),
                  pl.BlockSpec((t, LANES), lambda i: (i % per_seq, 0))],
        out_specs=[pl.BlockSpec((t, 2 * LANES), row), pl.BlockSpec((t, LANES), row), pl.BlockSpec((t, LANES), row)],
        out_shape=[jax.ShapeDtypeStruct((n, 2 * LANES), BF16), jax.ShapeDtypeStruct((n, LANES), BF16),
                   jax.ShapeDtypeStruct((n, LANES), BF16)],
        compiler_params=_cparams("parallel"),
        name="qkv_proj",
    )(hb, w, b, gain, scale, cos, sin)


def _zproj_kernel(hb_ref, wsc_ref, wcf_ref, wpl_ref, wg_ref, bsc_ref, bcf_ref, bpl_ref, bg_ref,
                  zsc_ref, zcf_ref, zpl_ref, zg_ref):
    x = hb_ref[...]
    for w_ref, b_ref, o_ref in ((wsc_ref, bsc_ref, zsc_ref), (wcf_ref, bcf_ref, zcf_ref),
                                (wpl_ref, bpl_ref, zpl_ref), (wg_ref, bg_ref, zg_ref)):
        o_ref[...] = (jnp.dot(x, w_ref[...], preferred_element_type=F32) + b_ref[...]).astype(BF16)


def _z_proj(hb, ws, bs):
    n, d = hb.shape
    t = _tile(n, 256)
    row = lambda i: (i, 0)
    widths = [w.shape[1] for w in ws]
    return pl.pallas_call(
        _zproj_kernel,
        grid=(n // t,),
        in_specs=[pl.BlockSpec((t, d), row)]
        + [pl.BlockSpec((d, c), lambda i: (0, 0), pipeline_mode=pl.Buffered(1)) for c in widths]
        + [_const_spec((1, c)) for c in widths],
        out_specs=[pl.BlockSpec((t, c), row) for c in widths],
        out_shape=[jax.ShapeDtypeStruct((n, c), BF16) for c in widths],
        compiler_params=_cparams("parallel"),
        name="z_proj",
    )(hb, *ws, *bs)


def _attn_kernel(q_ref, kt_ref, v_ref, o_ref, m_scr, acc_scr):
    g, tq, hd = q_ref.shape
    rows = g * tq
    n_chunks, _, tk = kt_ref.shape
    q = q_ref[...].reshape(rows, hd)
    m_scr[...] = jnp.full(m_scr.shape, -jnp.inf, F32)
    acc_scr[...] = jnp.zeros(acc_scr.shape, F32)

    def body(c, carry):
        s = jnp.dot(q, kt_ref[c], preferred_element_type=F32)
        m_prev = m_scr[...]
        m_new = jnp.maximum(m_prev, jnp.max(s, axis=-1, keepdims=True))
        alpha = jnp.exp2(m_prev - m_new)
        p = jnp.exp2(s - jnp.concatenate([m_new] * (tk // LANES), axis=1)).astype(BF16)
        acc_scr[...] = alpha * acc_scr[...] + jnp.dot(p, v_ref[c], preferred_element_type=F32)
        m_scr[...] = m_new
        return carry

    lax.fori_loop(0, n_chunks, body, 0, unroll=2 if n_chunks % 2 == 0 else 1)
    acc = acc_scr[...]
    res = acc / acc[:, hd:hd + 1]
    lane = lax.broadcasted_iota(I32, (tq, LANES), 1)
    o_ref[...] = jnp.where(lane < hd, res[:tq], pltpu.roll(res[tq:], hd, 1)).astype(BF16)


def _attention(q, k, v, batch, seq):
    tq = _tile(seq, 256)
    tk = _tile(seq, 2048)
    nc = seq // tk
    q5 = q.reshape(batch, seq, N_KV_HEADS, Q_GROUP, HEAD_DIM).transpose(0, 2, 3, 1, 4)
    kt = k.reshape(batch, nc, tk, N_KV_HEADS, HEAD_DIM).transpose(0, 3, 1, 4, 2)
    v5 = v.reshape(batch, nc, tk, N_KV_HEADS, HEAD_DIM).transpose(0, 3, 1, 2, 4)
    ones = jnp.ones(v5.shape[:-1] + (1,), BF16)
    zeros = jnp.zeros(v5.shape[:-1] + (LANES - HEAD_DIM - 1,), BF16)
    v_ext = jnp.concatenate([v5, ones, zeros], axis=-1)
    rows = Q_GROUP * tq
    return pl.pallas_call(
        _attn_kernel,
        grid=(batch, N_KV_HEADS, seq // tq),
        in_specs=[pl.BlockSpec((None, None, Q_GROUP, tq, HEAD_DIM), lambda b, h, i: (b, h, 0, i, 0)),
                  pl.BlockSpec((None, None, nc, HEAD_DIM, tk), lambda b, h, i: (b, h, 0, 0, 0)),
                  pl.BlockSpec((None, None, nc, tk, LANES), lambda b, h, i: (b, h, 0, 0, 0))],
        out_specs=pl.BlockSpec((None, tq, LANES), lambda b, h, i: (b, i, h)),
        out_shape=jax.ShapeDtypeStruct((batch, seq, N_KV_HEADS * LANES), BF16),
        scratch_shapes=[pltpu.VMEM((rows, LANES), F32), pltpu.VMEM((rows, LANES), F32)],
        compiler_params=_cparams("parallel", "parallel", "arbitrary"),
        name="attention",
    )(q5, kt, v_ext).reshape(batch * seq, N_KV_HEADS * LANES)


def _dwconv(xs_ref, w_ref, n_taps, first_row, rows):
    acc = None
    for j in range(n_taps):
        term = xs_ref[pl.ds(first_row + j, rows), :] * w_ref[j:j + 1, :]
        acc = term if acc is None else acc + term
    return acc


def _mixer_kernel(seq, alpha,
                  o_ref, scp_ref, scc_ref, scn_ref, cfp_ref, cfc_ref, cfn_ref, plp_ref, plc_ref, pln_ref,
                  zg_ref, h_ref, scw_ref, cfw_ref, cfb_ref, cfg_ref, cfbeta_ref, plmask_ref, pllo_ref, plhi_ref,
                  plw_ref, pls_ref, wbr_ref, wout_ref, ln1g_ref, ln1b_ref, wr_ref, br_ref,
                  h1_ref, ridx_ref, rw_ref, rrank_ref, cnt_ref,
                  xsc, xcf, xpl, cnt_scr):
    t = h_ref.shape[0]
    w = BRANCH_W
    i = pl.program_id(0)
    per_seq = seq // t
    j = i % per_seq
    has_prev = (j > 0).astype(F32)
    has_next = (j < per_seq - 1).astype(F32)

    def sc_v(z):
        z = z.astype(F32)
        return z[:, w:2 * w] * z[:, 2 * w:3 * w]
    xsc[0:HALO, :] = sc_v(scp_ref[...]) * has_prev
    xsc[HALO:HALO + t, :] = sc_v(scc_ref[...])
    xsc[HALO + t:2 * HALO + t, :] = sc_v(scn_ref[...]) * has_next
    y_sc = scc_ref[:, 0:w].astype(F32) * _dwconv(xsc, scw_ref, SC_WIDTH, HALO - SC_WIDTH // 2, t)

    def cf_v(z):
        z = z.astype(F32)
        return z[:, 0:w] * _sigmoid(z[:, w:2 * w])
    xcf[0:HALO, :] = cf_v(cfp_ref[...]) * has_prev
    xcf[HALO:HALO + t, :] = cf_v(cfc_ref[...])
    xcf[HALO + t:2 * HALO + t, :] = cf_v(cfn_ref[...]) * has_next
    y_cf = _dwconv(xcf, cfw_ref, CF_WIDTH, HALO - CF_WIDTH // 2, t) + cfb_ref[...]
    y_cf = _layer_norm_rows(y_cf, cfg_ref[...], cfbeta_ref[...])
    y_cf = y_cf * _sigmoid(y_cf)

    u = plc_ref[...].astype(F32)
    xpl[0:HALO, :] = plp_ref[...].astype(F32) * has_prev
    xpl[HALO:HALO + t, :] = u
    xpl[HALO + t:2 * HALO + t, :] = pln_ref[...].astype(F32) * has_next
    win_sum = _dwconv(xpl, plmask_ref, POOL_TAPS, HALO + POOL_FIRST, t)
    pos = (j * t + lax.broadcasted_iota(I32, (t, w), 0)).astype(F32)
    n_win = jnp.minimum(pos + plhi_ref[...], float(seq - 1)) - jnp.maximum(pos - pllo_ref[...], 0.0) + 1.0
    y_pl = win_sum / n_win - u
    y_pl = jnp.dot(y_pl.astype(BF16), plw_ref[...], preferred_element_type=F32) * pls_ref[...]

    merged = None
    for g, br in enumerate((o_ref[...], y_sc.astype(BF16), y_cf.astype(BF16), y_pl.astype(BF16))):
        proj = jnp.dot(br, wbr_ref[g], preferred_element_type=F32)
        gate = _sigmoid(zg_ref[:, g * proj.shape[1]:(g + 1) * proj.shape[1]].astype(F32))
        merged = gate * proj if merged is None else merged + gate * proj
    mix = jnp.dot(merged.astype(BF16), wout_ref[...], preferred_element_type=F32)
    h1 = _layer_norm_rows(alpha * h_ref[...] + mix, ln1g_ref[...], ln1b_ref[...])
    h1_ref[...] = h1

    logits = jnp.dot(h1.astype(BF16), wr_ref[...], preferred_element_type=F32) + br_ref[...]
    lane = lax.broadcasted_iota(I32, (t, LANES), 1)
    vals = logits
    tops, sels, idxs = [], [], []
    for _ in range(TOP_K):
        m = jnp.max(vals, axis=-1, keepdims=True)
        first = jnp.min(jnp.where(vals == m, lane, LANES), axis=-1, keepdims=True)
        sel = lane == first
        vals = jnp.where(sel, -jnp.inf, vals)
        tops.append(m)
        sels.append(sel)
        idxs.append(first)
    exps = [jnp.exp(m - tops[0]) for m in tops]
    denom = exps[0] + exps[1] + exps[2] + exps[3]

    @pl.when(i == 0)
    def _():
        cnt_scr[...] = jnp.zeros(cnt_scr.shape, F32)

    onehot = jnp.where(sels[0] | sels[1] | sels[2] | sels[3], 1.0, 0.0)
    r_id = lax.broadcasted_iota(I32, (t, t), 0)
    c_id = lax.broadcasted_iota(I32, (t, t), 1)
    lower = jnp.where(c_id < r_id, 1.0, 0.0).astype(BF16)
    rank_e = jnp.dot(lower, onehot.astype(BF16), preferred_element_type=F32) + cnt_scr[...]
    new_cnt = cnt_scr[...] + jnp.sum(onehot, axis=0, keepdims=True)
    cnt_scr[...] = new_cnt
    cnt_ref[...] = new_cnt.astype(I32)

    ridx = jnp.zeros((t, LANES), I32)
    rw = jnp.zeros((t, LANES), F32)
    rrank = jnp.zeros((t, LANES), I32)
    for k in range(TOP_K):
        rk = jnp.sum(jnp.where(sels[k], rank_e, 0.0), axis=-1, keepdims=True)
        ridx = jnp.where(lane == k, idxs[k], ridx)
        rw = jnp.where(lane == k, exps[k] / denom, rw)
        rrank = jnp.where(lane == k, rk.astype(I32), rrank)
    ridx_ref[...] = ridx
    rw_ref[...] = rw
    rrank_ref[...] = rrank


def _mixer(o, zsc, zcf, zpl, zg, h, p, seq, alpha):
    n, d = h.shape
    t = _tile(seq, 256)
    hb = t // HALO
    n_halo = n // HALO
    row = lambda i: (i, 0)
    prev = lambda i: (jnp.maximum(i * hb - 1, 0), 0)
    nxt = lambda i: (jnp.minimum((i + 1) * hb, n_halo - 1), 0)

    def halo_specs(c):
        return [pl.BlockSpec((HALO, c), prev), pl.BlockSpec((t, c), row), pl.BlockSpec((HALO, c), nxt)]

    consts = [p["sc_w"], p["cf_w"], p["cf_b"], p["cf_g"], p["cf_beta"], p["pl_mask"], p["pl_lo"], p["pl_hi"],
              p["pl_w"], p["pl_s"], p["w_br"], p["w_out"], p["ln1_g"], p["ln1_b"], p["w_r"], p["b_r"]]
    lane_out = lambda dt: jax.ShapeDtypeStruct((n, LANES), dt)
    return pl.pallas_call(
        functools.partial(_mixer_kernel, seq, alpha),
        grid=(n // t,),
        in_specs=[pl.BlockSpec((t, BRANCH_W), row)] + halo_specs(3 * BRANCH_W) + halo_specs(2 * BRANCH_W)
        + halo_specs(BRANCH_W) + [pl.BlockSpec((t, N_BRANCHES * d), row), pl.BlockSpec((t, d), row)]
        + [_const_spec(c.shape) for c in consts],
        out_specs=[pl.BlockSpec((t, d), row), pl.BlockSpec((t, LANES), row), pl.BlockSpec((t, LANES), row),
                   pl.BlockSpec((t, LANES), row), _const_spec((1, LANES))],
        out_shape=[jax.ShapeDtypeStruct((n, d), F32), lane_out(I32), lane_out(F32), lane_out(I32),
                   jax.ShapeDtypeStruct((1, LANES), I32)],
        scratch_shapes=[pltpu.VMEM((t + 2 * HALO, BRANCH_W), F32)] * 3 + [pltpu.VMEM((1, LANES), F32)],
        compiler_params=_cparams("arbitrary"),
        name="mixer",
    )(o, zsc, zsc, zsc, zcf, zcf, zcf, zpl, zpl, zpl, zg, h, *consts)


def _row_copy(src, src_row, dst, dst_row, sem):
    return pltpu.make_async_copy(src.at[pl.ds(src_row, 1), :], dst.at[pl.ds(dst_row, 1), :], sem)


def _dispatch_kernel(tile, pad_start_ref, pad_cnt_ref, n_used_ref, dest_hbm, x_ref, xs_hbm, idx_smem, zrow, idx_sem, row_sem, pad_sem):
    i = pl.program_id(0)
    n_steps = pl.num_programs(0)
    n_idx = tile * TOP_K

    def idx_copy(step, slot):
        return pltpu.make_async_copy(dest_hbm.at[pl.ds(step * n_idx, n_idx)], idx_smem.at[slot], idx_sem.at[slot])

    def wait_rows():
        pltpu.make_async_copy(xs_hbm.at[pl.ds(0, n_idx), :], xs_hbm.at[pl.ds(0, n_idx), :], row_sem).wait()

    @pl.when(i == 0)
    def _():
        idx_copy(0, 0).start()
        zrow[...] = jnp.zeros(zrow.shape, F32)
        n_blocks = xs_hbm.shape[0] // MOE_BLOCK

        def block_copy(b):
            return pltpu.make_async_copy(zrow, xs_hbm.at[pl.ds(b * MOE_BLOCK, MOE_BLOCK), :], pad_sem)

        def fill_block(b, c):
            block_copy(b).start()
            return c
        lax.fori_loop(n_used_ref[0], n_blocks, fill_block, 0)
        for e in range(N_EXPERTS):
            def fill(r, c, e=e):
                _row_copy(zrow, 0, xs_hbm, pad_start_ref[e] + r, pad_sem).start()
                return c
            lax.fori_loop(0, pad_cnt_ref[e], fill, 0)
        for e in range(N_EXPERTS):
            def drain(r, c, e=e):
                _row_copy(zrow, 0, xs_hbm, pad_start_ref[e] + r, pad_sem).wait()
                return c
            lax.fori_loop(0, pad_cnt_ref[e], drain, 0)

        def drain_block(b, c):
            block_copy(b).wait()
            return c
        lax.fori_loop(n_used_ref[0], n_blocks, drain_block, 0)

    slot = i % 2
    idx_copy(i, slot).wait()

    @pl.when(i + 1 < n_steps)
    def _():
        idx_copy(i + 1, 1 - slot).start()

    def issue(tok, c):
        for k in range(TOP_K):
            _row_copy(x_ref, tok, xs_hbm, idx_smem[slot, tok * TOP_K + k], row_sem).start()
        return c
    lax.fori_loop(0, tile, issue, 0, unroll=8)
    wait_rows()


def _dispatch(x, dest_flat, pad_start, pad_cnt, n_used, n_slots):
    n, d = x.shape
    tile = _tile(n, 512)
    return pl.pallas_call(
        functools.partial(_dispatch_kernel, tile),
        grid_spec=pltpu.PrefetchScalarGridSpec(
            num_scalar_prefetch=3,
            grid=(n // tile,),
            in_specs=[pl.BlockSpec(memory_space=pl.ANY), pl.BlockSpec((tile, d), lambda i, *_: (i, 0))],
            out_specs=pl.BlockSpec(memory_space=pl.ANY),
            scratch_shapes=[pltpu.SMEM((2, tile * TOP_K), I32), pltpu.VMEM((MOE_BLOCK, d), F32),
                            pltpu.SemaphoreType.DMA((2,)), pltpu.SemaphoreType.DMA, pltpu.SemaphoreType.DMA],
        ),
        out_shape=jax.ShapeDtypeStruct((n_slots, d), F32),
        compiler_params=pltpu.CompilerParams(dimension_semantics=("arbitrary",), has_side_effects=True),
        name="moe_dispatch",
    )(pad_start, pad_cnt, n_used, dest_flat, x)


def _expert_kernel(blk_e_ref, n_used_ref, xs_ref, wgu_ref, bgu_ref, wd_ref, bd_ref, y_ref):
    b = pl.program_id(0)

    @pl.when(b < n_used_ref[0])
    def _():
        f = wd_ref.shape[0]
        x = xs_ref[...].astype(BF16)
        hgu = jnp.dot(x, wgu_ref[...], preferred_element_type=F32) + bgu_ref[...]
        glu = jnp.minimum(hgu[:, :f], SWIGLU_LIMIT)
        lin = jnp.clip(hgu[:, f:], -SWIGLU_LIMIT, SWIGLU_LIMIT)
        act = glu * _sigmoid(SWIGLU_ALPHA * glu) * (lin + 1.0)
        y_ref[...] = jnp.dot(act.astype(BF16), wd_ref[...], preferred_element_type=F32) + bd_ref[...]

    @pl.when(b >= n_used_ref[0])
    def _():
        y_ref[...] = jnp.zeros(y_ref.shape, F32)


def _experts(xs, blk_e, n_used, wgu, bgu, wd, bd):
    n_slots, d = xs.shape
    nb = n_slots // MOE_BLOCK
    f2 = wgu.shape[2]
    last = lambda b, be, nu: (jnp.minimum(b, nu[0] - 1), 0)
    return pl.pallas_call(
        _expert_kernel,
        grid_spec=pltpu.PrefetchScalarGridSpec(
            num_scalar_prefetch=2,
            grid=(nb,),
            in_specs=[pl.BlockSpec((MOE_BLOCK, d), last),
                      pl.BlockSpec((None, d, f2), lambda b, be, nu: (be[b], 0, 0)),
                      pl.BlockSpec((None, 1, f2), lambda b, be, nu: (be[b], 0, 0)),
                      pl.BlockSpec((None, f2 // 2, d), lambda b, be, nu: (be[b], 0, 0)),
                      pl.BlockSpec((None, 1, d), lambda b, be, nu: (be[b], 0, 0))],
            out_specs=pl.BlockSpec((MOE_BLOCK, d), lambda b, be, nu: (b, 0)),
        ),
        out_shape=jax.ShapeDtypeStruct((n_slots, d), F32),
        compiler_params=_cparams("arbitrary"),
        name="moe_experts",
    )(blk_e, n_used, xs, wgu, bgu, wd, bd)


def _combine_kernel(tile, alpha, dest_hbm, y_hbm, rw_ref, h1_ref, g_ref, b_ref, h_ref, hb_ref,
                    idx_smem, buf, idx_sem, row_sem):
    i = pl.program_id(0)
    n_steps = pl.num_programs(0)
    n_idx = tile * TOP_K

    def idx_copy(step, slot):
        return pltpu.make_async_copy(dest_hbm.at[pl.ds(step * n_idx, n_idx)], idx_smem.at[slot], idx_sem.at[slot])

    def issue_rows(slot):
        def issue(tok, c):
            for k in range(TOP_K):
                pltpu.make_async_copy(y_hbm.at[pl.ds(idx_smem[slot, tok * TOP_K + k], 1), :],
                                      buf.at[slot, k, pl.ds(tok, 1), :], row_sem.at[slot]).start()
            return c
        lax.fori_loop(0, tile, issue, 0, unroll=8)

    slot = i % 2

    @pl.when(i == 0)
    def _():
        idx_copy(0, 0).start()
        idx_copy(0, 0).wait()
        issue_rows(0)

        @pl.when(n_steps > 1)
        def _():
            idx_copy(1, 1).start()

    @pl.when(i + 1 < n_steps)
    def _():
        idx_copy(i + 1, 1 - slot).wait()
        issue_rows(1 - slot)

    @pl.when(i + 2 < n_steps)
    def _():
        idx_copy(i + 2, slot).start()

    pltpu.make_async_copy(buf.at[slot], buf.at[slot], row_sem.at[slot]).wait()

    lane = lax.broadcasted_iota(I32, rw_ref.shape, 1)
    rw = rw_ref[...]
    ffn = None
    for k in range(TOP_K):
        wk = jnp.sum(jnp.where(lane == k, rw, 0.0), axis=-1, keepdims=True)
        term = wk * buf[slot, k]
        ffn = term if ffn is None else ffn + term
    h = _layer_norm_rows(alpha * h1_ref[...] + ffn, g_ref[...], b_ref[...])
    h_ref[...] = h
    hb_ref[...] = h.astype(BF16)


def _combine(y, dest_flat, rw, h1, g, b, alpha):
    n, d = h1.shape
    tile = _tile(n, 256)
    row = lambda i: (i, 0)
    return pl.pallas_call(
        functools.partial(_combine_kernel, tile, alpha),
        grid=(n // tile,),
        in_specs=[pl.BlockSpec(memory_space=pl.ANY), pl.BlockSpec(memory_space=pl.ANY),
                  pl.BlockSpec((tile, LANES), row), pl.BlockSpec((tile, d), row), _const_spec((1, d)), _const_spec((1, d))],
        out_specs=[pl.BlockSpec((tile, d), row), pl.BlockSpec((tile, d), row)],
        out_shape=[jax.ShapeDtypeStruct((n, d), F32), jax.ShapeDtypeStruct((n, d), BF16)],
        scratch_shapes=[pltpu.SMEM((2, tile * TOP_K), I32), pltpu.VMEM((2, TOP_K, tile, d), F32),
                        pltpu.SemaphoreType.DMA((2,)), pltpu.SemaphoreType.DMA((2,))],
        compiler_params=_cparams("arbitrary"),
        name="moe_combine",
    )(dest_flat, y, rw, h1, g.reshape(1, d), b.reshape(1, d))


def _rope_tables(seq):
    rows = seq // GRID_W
    row = jnp.repeat(jnp.arange(rows, dtype=F32), GRID_W)
    col = jnp.tile(jnp.arange(GRID_W, dtype=F32), rows)
    inv = ROPE_THETA ** (-jnp.arange(0, ROPE_HALF, 2, dtype=F32) / ROPE_HALF)
    ang_r = row[:, None] * inv
    ang_c = col[:, None] * inv
    cr, sr, cc, sc = jnp.cos(ang_r), jnp.sin(ang_r), jnp.cos(ang_c), jnp.sin(ang_c)
    cos = jnp.concatenate([cr, cr, cc, cc], axis=-1)
    sin = jnp.concatenate([-sr, sr, -sc, sc], axis=-1)
    return jnp.tile(cos, (1, 2)), jnp.tile(sin, (1, 2))


def _pool_constants():
    offs = np.arange(POOL_TAPS) + POOL_FIRST
    mask = np.zeros((POOL_TAPS, BRANCH_W), np.float32)
    lo = np.zeros((1, BRANCH_W), np.float32)
    hi = np.zeros((1, BRANCH_W), np.float32)
    for g, win in enumerate(POOL_WINDOWS):
        sl = slice(g * POOL_GROUP, (g + 1) * POOL_GROUP)
        mask[(offs >= -(win // 2)) & (offs <= win - 1 - win // 2), sl] = 1.0
        lo[0, sl] = win // 2
        hi[0, sl] = win - 1 - win // 2
    return jnp.asarray(mask), jnp.asarray(lo), jnp.asarray(hi)


def _block_diag(blocks):
    g, a, b = blocks.shape
    out = jnp.zeros((g * a, g * b), blocks.dtype)
    for i in range(g):
        out = out.at[i * a:(i + 1) * a, i * b:(i + 1) * b].set(blocks[i])
    return out


def kernel(x, ln_in_g, ln_in_b, w_in, b_in, q_norm_g, k_norm_g, sc_conv_w, cf_conv_w, cf_conv_b, cf_ln_g, cf_ln_b, pool_w, pool_scale, w_branch, w_out, ln1_g, ln1_b, w_router, b_router, w_gate_up, b_gate_up, w_down, b_down, ln2_g, ln2_b):
    batch, seq, d = x.shape
    depth = w_in.shape[0]
    n = batch * seq
    alpha = float((2.0 * depth) ** 0.25)
    n_assign = n * TOP_K
    n_slots = (n_assign + MOE_BLOCK - 1) // MOE_BLOCK * MOE_BLOCK + N_EXPERTS * MOE_BLOCK
    n_blocks = n_slots // MOE_BLOCK

    cos, sin = _rope_tables(seq)
    pl_mask, pl_lo, pl_hi = _pool_constants()
    q_scale = jnp.concatenate([jnp.full((1, 2 * LANES), HEAD_DIM ** -0.5 * np.log2(np.e), F32),
                               jnp.ones((1, LANES), F32)], axis=-1)
    splits = np.cumsum([QKV_COLS, 3 * BRANCH_W, 2 * BRANCH_W, BRANCH_W])

    h, hb = _entry_ln(x.reshape(n, d), ln_in_g, ln_in_b)
    for l in range(depth):
        wl = w_in[l].astype(BF16)
        bl = b_in[l].reshape(1, -1)
        w_parts = [wl[:, :splits[0]]] + [wl[:, splits[i]:splits[i + 1]] for i in range(3)] + [wl[:, splits[3]:]]
        b_parts = [bl[:, :splits[0]]] + [bl[:, splits[i]:splits[i + 1]] for i in range(3)] + [bl[:, splits[3]:]]
        gain = jnp.concatenate([jnp.tile(q_norm_g[l], N_Q_HEADS), jnp.tile(k_norm_g[l], N_KV_HEADS)]).reshape(1, -1)

        q, k, v = _qkv_proj(hb, w_parts[0], b_parts[0], gain, q_scale, cos, sin, seq)
        zsc, zcf, zpl, zg = _z_proj(hb, w_parts[1:], b_parts[1:])
        o = _attention(q, k, v, batch, seq)

        r_pad = LANES - N_EXPERTS
        params = dict(
            sc_w=sc_conv_w[l], cf_w=cf_conv_w[l], cf_b=cf_conv_b[l].reshape(1, -1), cf_g=cf_ln_g[l].reshape(1, -1),
            cf_beta=cf_ln_b[l].reshape(1, -1), pl_mask=pl_mask, pl_lo=pl_lo, pl_hi=pl_hi,
            pl_w=_block_diag(pool_w[l]).astype(BF16), pl_s=pool_scale[l].reshape(1, -1),
            w_br=w_branch[l].astype(BF16), w_out=w_out[l].astype(BF16),
            ln1_g=ln1_g[l].reshape(1, -1), ln1_b=ln1_b[l].reshape(1, -1),
            w_r=jnp.pad(w_router[l], ((0, 0), (0, r_pad))).astype(BF16),
            b_r=jnp.pad(b_router[l].astype(F32), (0, r_pad), constant_values=NEG_BIG).reshape(1, -1),
        )
        h1, ridx, rw, rrank, cnt = _mixer(o, zsc, zcf, zpl, zg, h, params, seq, alpha)

        counts = cnt[0, :N_EXPERTS]
        padded = (counts + MOE_BLOCK - 1) // MOE_BLOCK * MOE_BLOCK
        pend = jnp.cumsum(padded)
        pstart = pend - padded
        dest = (pstart[ridx[:, :TOP_K]] + rrank[:, :TOP_K]).reshape(-1).astype(I32)
        block_start = jnp.arange(n_blocks, dtype=I32) * MOE_BLOCK
        blk_e = jnp.minimum(jnp.sum(pend[None, :] <= block_start[:, None], axis=1), N_EXPERTS - 1).astype(I32)
        n_used = (pend[-1:] // MOE_BLOCK).astype(I32)

        xs = _dispatch(h1, dest, (pstart + counts).astype(I32), (padded - counts).astype(I32), n_used, n_slots)
        y = _experts(xs, blk_e, n_used, w_gate_up[l].astype(BF16), b_gate_up[l][:, None, :],
                     w_down[l].astype(BF16), b_down[l][:, None, :])
        h, hb = _combine(y, dest, rw, h1, ln2_g[l], ln2_b[l], alpha)
    return h.reshape(batch, seq, d)
```

```python
import functools

import jax
import jax.numpy as jnp
import numpy as np
from jax import lax
from jax.experimental import pallas as pl
from jax.experimental.pallas import tpu as pltpu

F32 = jnp.float32
BF16 = jnp.bfloat16
I32 = jnp.int32

GRID_W = 64
HEAD_DIM = 64
N_Q_HEADS = 4
N_KV_HEADS = 2
Q_GROUP = N_Q_HEADS // N_KV_HEADS
ROPE_HALF = HEAD_DIM // 2
ROPE_THETA = 10000.0
BRANCH_W = 256
N_BRANCHES = 4
SC_WIDTH = 3
CF_WIDTH = 31
POOL_WINDOWS = (2, 4, 8, 16)
POOL_GROUP = BRANCH_W // len(POOL_WINDOWS)
POOL_TAPS = 16
POOL_FIRST = -8
N_EXPERTS = 32
TOP_K = 4
SWIGLU_LIMIT = 7.0
SWIGLU_ALPHA = 1.702
LN_EPS = 1e-5
RMS_EPS = 1e-6
QKV_COLS = N_Q_HEADS * HEAD_DIM + 2 * N_KV_HEADS * HEAD_DIM

LANES = 128
HALO = 16
ROW_GROUP = 8
MOE_TILE = 256
MOE_BLOCK = 512
VMEM_LIMIT = 56 * 1024 * 1024
NEG_BIG = -1e30


def _tile_sorted_rows(t):
    return t * TOP_K + N_EXPERTS * ROW_GROUP


def _cparams(*sem):
    return pltpu.CompilerParams(dimension_semantics=sem, vmem_limit_bytes=VMEM_LIMIT)


def _tile(n, pref):
    t = min(n, pref)
    assert n % t == 0, (n, t)
    return t


def _const_spec(shape):
    nd = len(shape)
    return pl.BlockSpec(shape, lambda *_: (0,) * nd)


def _layer_norm_rows(y, g, b):
    mu = jnp.mean(y, axis=-1, keepdims=True)
    d = y - mu
    var = jnp.mean(d * d, axis=-1, keepdims=True)
    return d * lax.rsqrt(var + LN_EPS) * g + b


def _sigmoid(x):
    return 1.0 / (1.0 + jnp.exp(-x))


def _ln_kernel(x_ref, g_ref, b_ref, h_ref, hb_ref):
    h = _layer_norm_rows(x_ref[...], g_ref[...], b_ref[...])
    h_ref[...] = h
    hb_ref[...] = h.astype(BF16)


def _entry_ln(x2, g, b):
    n, d = x2.shape
    t = _tile(n, 512)
    return pl.pallas_call(
        _ln_kernel,
        grid=(n // t,),
        in_specs=[pl.BlockSpec((t, d), lambda i: (i, 0)), _const_spec((1, d)), _const_spec((1, d))],
        out_specs=[pl.BlockSpec((t, d), lambda i: (i, 0)), pl.BlockSpec((t, d), lambda i: (i, 0))],
        out_shape=[jax.ShapeDtypeStruct((n, d), F32), jax.ShapeDtypeStruct((n, d), BF16)],
        compiler_params=_cparams("parallel"),
        name="entry_ln",
    )(x2, g.reshape(1, d), b.reshape(1, d))


def _qkv_kernel(hb_ref, w_ref, b_ref, gain_ref, scale_ref, cos_ref, sin_ref, q_ref, kt_ref, v_ref):
    z = jnp.dot(hb_ref[...], w_ref[...], preferred_element_type=F32) + b_ref[...]
    t = z.shape[0]
    lane = lax.broadcasted_iota(I32, (t, LANES), 1)
    low_head = lane < HEAD_DIM
    first_half = (lane % ROPE_HALF) < (ROPE_HALF // 2)
    cos = cos_ref[...]
    sin = sin_ref[...]
    outs = []
    for c in range(3):
        x = z[:, c * LANES:(c + 1) * LANES]
        x2 = x * x
        s_all = jnp.sum(x2, axis=-1, keepdims=True)
        s_low = jnp.sum(jnp.where(low_head, x2, 0.0), axis=-1, keepdims=True)
        ms = jnp.where(low_head, s_low, s_all - s_low) * (1.0 / HEAD_DIM)
        xn = x * lax.rsqrt(ms + RMS_EPS) * gain_ref[:, c * LANES:(c + 1) * LANES]
        partner = jnp.where(first_half, pltpu.roll(xn, LANES - ROPE_HALF // 2, 1), pltpu.roll(xn, ROPE_HALF // 2, 1))
        outs.append((xn * cos + partner * sin) * scale_ref[:, c * LANES:(c + 1) * LANES])
    q_ref[:, 0:LANES] = outs[0].astype(BF16)
    q_ref[:, LANES:2 * LANES] = outs[1].astype(BF16)
    kt_ref[...] = outs[2].T.astype(BF16)
    v = z[:, 3 * LANES:4 * LANES]
    tail = jnp.where(lane == HEAD_DIM, 1.0, 0.0)
    v_ref[0] = jnp.where(low_head, v, tail).astype(BF16)
    v_ref[1] = jnp.where(low_head, pltpu.roll(v, HEAD_DIM, 1), tail).astype(BF16)


def _attn_tiles(seq):
    return _tile(seq, 256), _tile(seq, 2048)


def _qkv_proj(hb, w, b, gain, scale, cos, sin, batch, seq):
    n, d = hb.shape
    t = _tile(seq, 512)
    tk = _attn_tiles(seq)[1]
    assert tk % t == 0
    per_seq, per_chunk, nc = seq // t, tk // t, seq // tk
    row = lambda i: (i, 0)
    return pl.pallas_call(
        _qkv_kernel,
        grid=(n // t,),
        in_specs=[pl.BlockSpec((t, d), row), _const_spec((d, QKV_COLS)), _const_spec((1, QKV_COLS)),
                  _const_spec((1, 3 * LANES)), _const_spec((1, 3 * LANES)),
                  pl.BlockSpec((t, LANES), lambda i: (i % per_seq, 0)),
                  pl.BlockSpec((t, LANES), lambda i: (i % per_seq, 0))],
        out_specs=[pl.BlockSpec((t, 2 * LANES), row),
                   pl.BlockSpec((None, None, LANES, t),
                                lambda i: (i // per_seq, (i % per_seq) // per_chunk, 0, i % per_chunk)),
                   pl.BlockSpec((None, None, N_KV_HEADS, t, LANES),
                                lambda i: (i // per_seq, (i % per_seq) // per_chunk, 0, i % per_chunk, 0))],
        out_shape=[jax.ShapeDtypeStruct((n, 2 * LANES), BF16),
                   jax.ShapeDtypeStruct((batch, nc, LANES, tk), BF16),
                   jax.ShapeDtypeStruct((batch, nc, N_KV_HEADS, tk, LANES), BF16)],
        compiler_params=_cparams("parallel"),
        name="qkv_proj",
    )(hb, w, b, gain, scale, cos, sin)


def _zproj_kernel(hb_ref, wsc_ref, wcf_ref, wpl_ref, wg_ref, bsc_ref, bcf_ref, bpl_ref, bg_ref,
                  zsc_ref, zcf_ref, zpl_ref, zg_ref):
    x = hb_ref[...]
    for w_ref, b_ref, o_ref in ((wsc_ref, bsc_ref, zsc_ref), (wcf_ref, bcf_ref, zcf_ref),
                                (wpl_ref, bpl_ref, zpl_ref), (wg_ref, bg_ref, zg_ref)):
        o_ref[...] = (jnp.dot(x, w_ref[...], preferred_element_type=F32) + b_ref[...]).astype(BF16)


def _z_proj(hb, ws, bs):
    n, d = hb.shape
    t = _tile(n, 256)
    row = lambda i: (i, 0)
    widths = [w.shape[1] for w in ws]
    return pl.pallas_call(
        _zproj_kernel,
        grid=(n // t,),
        in_specs=[pl.BlockSpec((t, d), row)]
        + [pl.BlockSpec((d, c), lambda i: (0, 0), pipeline_mode=pl.Buffered(1)) for c in widths]
        + [_const_spec((1, c)) for c in widths],
        out_specs=[pl.BlockSpec((t, c), row) for c in widths],
        out_shape=[jax.ShapeDtypeStruct((n, c), BF16) for c in widths],
        compiler_params=_cparams("parallel"),
        name="z_proj",
    )(hb, *ws, *bs)


def _attn_kernel(q_ref, kt_ref, v_ref, o_ref, m_scr, acc_scr):
    h = pl.program_id(1)
    tq = q_ref.shape[0]
    rows = Q_GROUP * tq
    n_chunks, _, tk = kt_ref.shape
    lane = lax.broadcasted_iota(I32, (tq, LANES), 1)
    own_half = (lane // HEAD_DIM) == h
    first = h == 0
    qf = q_ref[...].astype(F32)
    swapped = pltpu.roll(qf, HEAD_DIM, 1)
    q = jnp.concatenate([jnp.where(own_half, jnp.where(first, qf, swapped), 0.0),
                         jnp.where(own_half, jnp.where(first, swapped, qf), 0.0)], axis=0).astype(BF16)
    m_scr[...] = jnp.full(m_scr.shape, -jnp.inf, F32)
    acc_scr[...] = jnp.zeros(acc_scr.shape, F32)

    def body(c, carry):
        s = jnp.dot(q, kt_ref[c], preferred_element_type=F32)
        m_prev = m_scr[...]
        m_new = jnp.maximum(m_prev, jnp.max(s, axis=-1, keepdims=True))
        alpha = jnp.exp2(m_prev - m_new)
        p = jnp.exp2(s - jnp.concatenate([m_new] * (tk // LANES), axis=1)).astype(BF16)
        acc_scr[...] = alpha * acc_scr[...] + jnp.dot(p, v_ref[c], preferred_element_type=F32)
        m_scr[...] = m_new
        return carry

    lax.fori_loop(0, n_chunks, body, 0, unroll=2 if n_chunks % 2 == 0 else 1)
    acc = acc_scr[...]
    res = acc / acc[:, HEAD_DIM:HEAD_DIM + 1]
    o_ref[...] = jnp.where(lane < HEAD_DIM, res[:tq], pltpu.roll(res[tq:], HEAD_DIM, 1)).astype(BF16)


def _attention(q, kt, v, batch, seq):
    tq, tk = _attn_tiles(seq)
    nc = seq // tk
    per_seq = seq // tq
    rows = Q_GROUP * tq
    return pl.pallas_call(
        _attn_kernel,
        grid=(batch, N_KV_HEADS, per_seq),
        in_specs=[pl.BlockSpec((tq, LANES), lambda b, h, i: (b * per_seq + i, h)),
                  pl.BlockSpec((None, nc, LANES, tk), lambda b, h, i: (b, 0, 0, 0)),
                  pl.BlockSpec((None, nc, None, tk, LANES), lambda b, h, i: (b, 0, h, 0, 0))],
        out_specs=pl.BlockSpec((None, tq, LANES), lambda b, h, i: (b, i, h)),
        out_shape=jax.ShapeDtypeStruct((batch, seq, N_KV_HEADS * LANES), BF16),
        scratch_shapes=[pltpu.VMEM((rows, LANES), F32), pltpu.VMEM((rows, LANES), F32)],
        compiler_params=_cparams("parallel", "parallel", "arbitrary"),
        name="attention",
    )(q, kt, v).reshape(batch * seq, N_KV_HEADS * LANES)


def _dwconv(xs_ref, w_ref, n_taps, first_row, rows):
    acc = None
    for j in range(n_taps):
        term = xs_ref[pl.ds(first_row + j, rows), :] * w_ref[j:j + 1, :]
        acc = term if acc is None else acc + term
    return acc


def _mixer_kernel(seq, alpha,
                  o_ref, scp_ref, scc_ref, scn_ref, cfp_ref, cfc_ref, cfn_ref, plp_ref, plc_ref, pln_ref,
                  zg_ref, h_ref, scw_ref, cfw_ref, cfb_ref, cfg_ref, cfbeta_ref, plmask_ref, pllo_ref, plhi_ref,
                  plw_ref, pls_ref, wbr_ref, wout_ref, ln1g_ref, ln1b_ref, wr_ref, br_ref,
                  h1_ref, ridx_ref, rw_ref, rpos_ref, cnt_ref, zt_ref,
                  xsc, xcf, xpl):
    t = h_ref.shape[0]
    w = BRANCH_W
    i = pl.program_id(0)
    per_seq = seq // t
    j = i % per_seq
    has_prev = (j > 0).astype(F32)
    has_next = (j < per_seq - 1).astype(F32)

    def sc_v(z):
        z = z.astype(F32)
        return z[:, w:2 * w] * z[:, 2 * w:3 * w]
    xsc[0:HALO, :] = sc_v(scp_ref[...]) * has_prev
    xsc[HALO:HALO + t, :] = sc_v(scc_ref[...])
    xsc[HALO + t:2 * HALO + t, :] = sc_v(scn_ref[...]) * has_next
    y_sc = scc_ref[:, 0:w].astype(F32) * _dwconv(xsc, scw_ref, SC_WIDTH, HALO - SC_WIDTH // 2, t)

    def cf_v(z):
        z = z.astype(F32)
        return z[:, 0:w] * _sigmoid(z[:, w:2 * w])
    xcf[0:HALO, :] = cf_v(cfp_ref[...]) * has_prev
    xcf[HALO:HALO + t, :] = cf_v(cfc_ref[...])
    xcf[HALO + t:2 * HALO + t, :] = cf_v(cfn_ref[...]) * has_next
    y_cf = _dwconv(xcf, cfw_ref, CF_WIDTH, HALO - CF_WIDTH // 2, t) + cfb_ref[...]
    y_cf = _layer_norm_rows(y_cf, cfg_ref[...], cfbeta_ref[...])
    y_cf = y_cf * _sigmoid(y_cf)

    u = plc_ref[...].astype(F32)
    xpl[0:HALO, :] = plp_ref[...].astype(F32) * has_prev
    xpl[HALO:HALO + t, :] = u
    xpl[HALO + t:2 * HALO + t, :] = pln_ref[...].astype(F32) * has_next
    win_sum = _dwconv(xpl, plmask_ref, POOL_TAPS, HALO + POOL_FIRST, t)
    pos = (j * t + lax.broadcasted_iota(I32, (t, w), 0)).astype(F32)
    n_win = jnp.minimum(pos + plhi_ref[...], float(seq - 1)) - jnp.maximum(pos - pllo_ref[...], 0.0) + 1.0
    y_pl = win_sum / n_win - u
    y_pl = jnp.dot(y_pl.astype(BF16), plw_ref[...], preferred_element_type=F32) * pls_ref[...]

    merged = None
    for g, br in enumerate((o_ref[...], y_sc.astype(BF16), y_cf.astype(BF16), y_pl.astype(BF16))):
        proj = jnp.dot(br, wbr_ref[g], preferred_element_type=F32)
        gate = _sigmoid(zg_ref[:, g * proj.shape[1]:(g + 1) * proj.shape[1]].astype(F32))
        merged = gate * proj if merged is None else merged + gate * proj
    mix = jnp.dot(merged.astype(BF16), wout_ref[...], preferred_element_type=F32)
    h1 = _layer_norm_rows(alpha * h_ref[...] + mix, ln1g_ref[...], ln1b_ref[...])
    h1_ref[...] = h1

    logits = jnp.dot(h1.astype(BF16), wr_ref[...], preferred_element_type=F32) + br_ref[...]
    lane = lax.broadcasted_iota(I32, (t, LANES), 1)
    vals = logits
    tops, sels, idxs = [], [], []
    for _ in range(TOP_K):
        m = jnp.max(vals, axis=-1, keepdims=True)
        first = jnp.min(jnp.where(vals == m, lane, LANES), axis=-1, keepdims=True)
        sel = lane == first
        vals = jnp.where(sel, -jnp.inf, vals)
        tops.append(m)
        sels.append(sel)
        idxs.append(first)
    exps = [jnp.exp(m - tops[0]) for m in tops]
    denom = exps[0] + exps[1] + exps[2] + exps[3]

    onehot = jnp.where(sels[0] | sels[1] | sels[2] | sels[3], 1.0, 0.0)
    r_id = lax.broadcasted_iota(I32, (t, t), 0)
    c_id = lax.broadcasted_iota(I32, (t, t), 1)
    lower = jnp.where(c_id < r_id, 1.0, 0.0).astype(BF16)
    rank_e = jnp.dot(lower, onehot.astype(BF16), preferred_element_type=F32)
    cnt = jnp.sum(onehot, axis=0, keepdims=True)
    groups = jnp.floor((cnt + (ROW_GROUP - 1.0)) * (1.0 / ROW_GROUP))
    e_r = lax.broadcasted_iota(I32, (LANES, LANES), 0)
    e_c = lax.broadcasted_iota(I32, (LANES, LANES), 1)
    before = jnp.where(e_r < e_c, 1.0, 0.0).astype(BF16)
    strip_start = jnp.dot(jnp.broadcast_to(groups, (ROW_GROUP, LANES)).astype(BF16), before,
                          preferred_element_type=F32)[0:1, :]
    row_e = rank_e + ROW_GROUP * strip_start
    cnt_ref[...] = jnp.broadcast_to(cnt, cnt_ref.shape).astype(I32)

    zr = zt_ref.shape[0]
    z_row = lax.broadcasted_iota(I32, (t, zr), 1)
    ridx = jnp.zeros((t, LANES), I32)
    rw = jnp.zeros((t, LANES), F32)
    rpos = jnp.zeros((t, LANES), I32)
    placed = None
    for k in range(TOP_K):
        pos_k = jnp.sum(jnp.where(sels[k], row_e, 0.0), axis=-1, keepdims=True).astype(I32)
        hit = z_row == pos_k
        placed = hit if placed is None else placed | hit
        ridx = jnp.where(lane == k, idxs[k], ridx)
        rw = jnp.where(lane == k, exps[k] / denom, rw)
        rpos = jnp.where(lane == k, pos_k, rpos)
    ridx_ref[...] = ridx
    rw_ref[...] = rw
    rpos_ref[...] = rpos
    scatter = jnp.where(placed, 1.0, 0.0).astype(BF16)
    zt_ref[...] = lax.dot_general(scatter, h1.astype(BF16), (((0,), (0,)), ((), ())), preferred_element_type=F32)


def _mixer(o, zsc, zcf, zpl, zg, h, p, seq, alpha):
    n, d = h.shape
    t = _tile(seq, MOE_TILE)
    hb = t // HALO
    n_halo = n // HALO
    row = lambda i: (i, 0)
    prev = lambda i: (jnp.maximum(i * hb - 1, 0), 0)
    nxt = lambda i: (jnp.minimum((i + 1) * hb, n_halo - 1), 0)

    def halo_specs(c):
        return [pl.BlockSpec((HALO, c), prev), pl.BlockSpec((t, c), row), pl.BlockSpec((HALO, c), nxt)]

    consts = [p["sc_w"], p["cf_w"], p["cf_b"], p["cf_g"], p["cf_beta"], p["pl_mask"], p["pl_lo"], p["pl_hi"],
              p["pl_w"], p["pl_s"], p["w_br"], p["w_out"], p["ln1_g"], p["ln1_b"], p["w_r"], p["b_r"]]
    lane_out = lambda dt: jax.ShapeDtypeStruct((n, LANES), dt)
    n_tiles = n // t
    zr = _tile_sorted_rows(t)
    return pl.pallas_call(
        functools.partial(_mixer_kernel, seq, alpha),
        grid=(n_tiles,),
        in_specs=[pl.BlockSpec((t, BRANCH_W), row)] + halo_specs(3 * BRANCH_W) + halo_specs(2 * BRANCH_W)
        + halo_specs(BRANCH_W) + [pl.BlockSpec((t, N_BRANCHES * d), row), pl.BlockSpec((t, d), row)]
        + [_const_spec(c.shape) for c in consts],
        out_specs=[pl.BlockSpec((t, d), row), pl.BlockSpec((t, LANES), row), pl.BlockSpec((t, LANES), row),
                   pl.BlockSpec((t, LANES), row), pl.BlockSpec((None, ROW_GROUP, LANES), lambda i: (i, 0, 0)),
                   pl.BlockSpec((zr, d), row)],
        out_shape=[jax.ShapeDtypeStruct((n, d), F32), lane_out(I32), lane_out(F32), lane_out(I32),
                   jax.ShapeDtypeStruct((n_tiles, ROW_GROUP, LANES), I32),
                   jax.ShapeDtypeStruct((n_tiles * zr, d), F32)],
        scratch_shapes=[pltpu.VMEM((t + 2 * HALO, BRANCH_W), F32)] * 3,
        compiler_params=_cparams("parallel"),
        name="mixer",
    )(o, zsc, zsc, zsc, zcf, zcf, zcf, zpl, zpl, zpl, zg, h, *consts)


def _group_copy(src_hbm, group, dst, dst_group, sem):
    return pltpu.make_async_copy(src_hbm.at[pl.ds(pl.multiple_of(group * ROW_GROUP, ROW_GROUP), ROW_GROUP), :],
                                 dst.at[pl.ds(dst_group * ROW_GROUP, ROW_GROUP), :], sem)


def _cast_rows(src_ref, dst_ref, chunk):
    def body(r, c):
        rows = pl.ds(pl.multiple_of(r * chunk, chunk), chunk)
        dst_ref[rows, :] = src_ref[rows, :].astype(dst_ref.dtype)
        return c
    lax.fori_loop(0, src_ref.shape[0] // chunk, body, 0)


def _expert_kernel(blk_e_ref, n_used_ref, src_ref, zt_hbm, wgu32_ref, bgu_ref, wd32_ref, bd_ref, y_ref,
                   xbuf, wgu_ref, wd_ref, sem):
    b = pl.program_id(0)
    n_used = n_used_ref[0]
    groups = MOE_BLOCK // ROW_GROUP

    @pl.when((b < n_used) & ((b == 0) | (blk_e_ref[b] != blk_e_ref[jnp.maximum(b - 1, 0)])))
    def _():
        _cast_rows(wgu32_ref, wgu_ref, LANES)
        _cast_rows(wd32_ref, wd_ref, LANES)

    def gather(block, slot):
        for g in range(groups):
            _group_copy(zt_hbm, src_ref[block * groups + g], xbuf.at[slot], g, sem.at[slot]).start()

    @pl.when(b == 0)
    def _():
        gather(0, 0)

    @pl.when(b + 1 < n_used)
    def _():
        gather(b + 1, (b + 1) % 2)

    @pl.when(b < n_used)
    def _():
        slot = b % 2
        pltpu.make_async_copy(xbuf.at[slot], xbuf.at[slot], sem.at[slot]).wait()
        f = wd_ref.shape[0]
        x = xbuf[slot].astype(BF16)
        hgu = jnp.dot(x, wgu_ref[...], preferred_element_type=F32) + bgu_ref[...]
        glu = jnp.minimum(hgu[:, :f], SWIGLU_LIMIT)
        lin = jnp.clip(hgu[:, f:], -SWIGLU_LIMIT, SWIGLU_LIMIT)
        act = glu * _sigmoid(SWIGLU_ALPHA * glu) * (lin + 1.0)
        y_ref[...] = jnp.dot(act.astype(BF16), wd_ref[...], preferred_element_type=F32) + bd_ref[...]

    @pl.when(b >= n_used)
    def _():
        y_ref[...] = jnp.zeros(y_ref.shape, F32)


def _experts(zt, blk_e, n_used, src_groups, n_blocks, layer, wgu, bgu, wd, bd):
    d = zt.shape[1]
    f2 = wgu.shape[3]
    expert = lambda b, be, nu, sg: (layer, be[b], 0, 0)
    return pl.pallas_call(
        _expert_kernel,
        grid_spec=pltpu.PrefetchScalarGridSpec(
            num_scalar_prefetch=3,
            grid=(n_blocks,),
            in_specs=[pl.BlockSpec(memory_space=pl.ANY),
                      pl.BlockSpec((None, None, d, f2), expert), pl.BlockSpec((None, None, 1, f2), expert),
                      pl.BlockSpec((None, None, f2 // 2, d), expert), pl.BlockSpec((None, None, 1, d), expert)],
            out_specs=pl.BlockSpec((MOE_BLOCK, d), lambda b, be, nu, sg: (b, 0)),
            scratch_shapes=[pltpu.VMEM((2, MOE_BLOCK, d), F32), pltpu.VMEM((d, f2), BF16),
                            pltpu.VMEM((f2 // 2, d), BF16), pltpu.SemaphoreType.DMA((2,))],
        ),
        out_shape=jax.ShapeDtypeStruct((n_blocks * MOE_BLOCK, d), F32),
        compiler_params=_cparams("arbitrary"),
        name="moe_experts",
    )(blk_e, n_used, src_groups, zt, wgu, bgu, wd, bd)


def _combine_kernel(alpha, src_ref, y_hbm, rw_ref, rpos_ref, h1_ref, g_ref, b_ref, h_ref, hb_ref, buf, sem):
    i = pl.program_id(0)
    n_steps = pl.num_programs(0)
    zr = buf.shape[1]
    groups = zr // ROW_GROUP
    t = h1_ref.shape[0]

    def gather(tile, slot):
        for g in range(groups):
            _group_copy(y_hbm, src_ref[tile * groups + g], buf.at[slot], g, sem.at[slot]).start()

    @pl.when(i == 0)
    def _():
        gather(0, 0)

    @pl.when(i + 1 < n_steps)
    def _():
        gather(i + 1, (i + 1) % 2)

    slot = i % 2
    pltpu.make_async_copy(buf.at[slot], buf.at[slot], sem.at[slot]).wait()

    lane = lax.broadcasted_iota(I32, (t, LANES), 1)
    z_row = lax.broadcasted_iota(I32, (t, zr), 1)
    rw = rw_ref[...]
    rpos = rpos_ref[...]
    weights = jnp.zeros((t, zr), F32)
    for k in range(TOP_K):
        w_k = jnp.sum(jnp.where(lane == k, rw, 0.0), axis=-1, keepdims=True)
        pos_k = jnp.sum(jnp.where(lane == k, rpos, 0), axis=-1, keepdims=True)
        weights = jnp.where(z_row == pos_k, w_k, weights)
    ffn = jnp.dot(weights.astype(BF16), buf[slot].astype(BF16), preferred_element_type=F32)
    h = _layer_norm_rows(alpha * h1_ref[...] + ffn, g_ref[...], b_ref[...])
    h_ref[...] = h
    hb_ref[...] = h.astype(BF16)


def _combine(y, src_groups, rw, rpos, h1, g, b, alpha, tile):
    n, d = h1.shape
    zr = _tile_sorted_rows(tile)
    row = lambda i, sg: (i, 0)
    const = lambda i, sg: (0, 0)
    return pl.pallas_call(
        functools.partial(_combine_kernel, alpha),
        grid_spec=pltpu.PrefetchScalarGridSpec(
            num_scalar_prefetch=1,
            grid=(n // tile,),
            in_specs=[pl.BlockSpec(memory_space=pl.ANY), pl.BlockSpec((tile, LANES), row),
                      pl.BlockSpec((tile, LANES), row), pl.BlockSpec((tile, d), row),
                      pl.BlockSpec((1, d), const), pl.BlockSpec((1, d), const)],
            out_specs=[pl.BlockSpec((tile, d), row), pl.BlockSpec((tile, d), row)],
            scratch_shapes=[pltpu.VMEM((2, zr, d), F32), pltpu.SemaphoreType.DMA((2,))],
        ),
        out_shape=[jax.ShapeDtypeStruct((n, d), F32), jax.ShapeDtypeStruct((n, d), BF16)],
        compiler_params=_cparams("arbitrary"),
        name="moe_combine",
    )(src_groups, y, rw, rpos, h1, g.reshape(1, d), b.reshape(1, d))


def _rope_tables(seq):
    rows = seq // GRID_W
    row = jnp.repeat(jnp.arange(rows, dtype=F32), GRID_W)
    col = jnp.tile(jnp.arange(GRID_W, dtype=F32), rows)
    inv = ROPE_THETA ** (-jnp.arange(0, ROPE_HALF, 2, dtype=F32) / ROPE_HALF)
    ang_r = row[:, None] * inv
    ang_c = col[:, None] * inv
    cr, sr, cc, sc = jnp.cos(ang_r), jnp.sin(ang_r), jnp.cos(ang_c), jnp.sin(ang_c)
    cos = jnp.concatenate([cr, cr, cc, cc], axis=-1)
    sin = jnp.concatenate([-sr, sr, -sc, sc], axis=-1)
    return jnp.tile(cos, (1, 2)), jnp.tile(sin, (1, 2))


def _pool_constants():
    offs = np.arange(POOL_TAPS) + POOL_FIRST
    mask = np.zeros((POOL_TAPS, BRANCH_W), np.float32)
    lo = np.zeros((1, BRANCH_W), np.float32)
    hi = np.zeros((1, BRANCH_W), np.float32)
    for g, win in enumerate(POOL_WINDOWS):
        sl = slice(g * POOL_GROUP, (g + 1) * POOL_GROUP)
        mask[(offs >= -(win // 2)) & (offs <= win - 1 - win // 2), sl] = 1.0
        lo[0, sl] = win // 2
        hi[0, sl] = win - 1 - win // 2
    return jnp.asarray(mask), jnp.asarray(lo), jnp.asarray(hi)


def _block_diag(blocks):
    g, a, b = blocks.shape
    out = jnp.zeros((g * a, g * b), blocks.dtype)
    for i in range(g):
        out = out.at[i * a:(i + 1) * a, i * b:(i + 1) * b].set(blocks[i])
    return out


def _moe_tables(tile_counts, tile, n_blocks):
    nt = tile_counts.shape[0]
    bg = MOE_BLOCK // ROW_GROUP
    gz = _tile_sorted_rows(tile) // ROW_GROUP
    ng = (tile_counts + ROW_GROUP - 1) // ROW_GROUP
    strip_end = jnp.cumsum(ng, axis=1)
    strip_start = strip_end - ng
    run_end = jnp.cumsum(ng, axis=0)
    run_start = run_end - ng
    total = run_end[-1]
    blocks_e = (total + bg - 1) // bg
    blk_end = jnp.cumsum(blocks_e)
    blk_start = blk_end - blocks_e
    n_used = blk_end[-1:].astype(I32)
    b_ids = jnp.arange(n_blocks, dtype=I32)
    blk_e = jnp.minimum(jnp.sum(blk_end[None, :] <= b_ids[:, None], axis=1), N_EXPERTS - 1).astype(I32)

    q = (b_ids - blk_start[blk_e])[:, None] * bg + jnp.arange(bg, dtype=I32)[None, :]
    ends = run_end.T[blk_e]
    src_tile = jnp.minimum(jnp.sum(ends[:, None, :] <= q[:, :, None], axis=2), nt - 1)
    e_b = blk_e[:, None]
    z_src = jnp.where(q < total[blk_e][:, None],
                      src_tile * gz + strip_start[src_tile, e_b] + q - run_start[src_tile, e_b], 0)

    j = jnp.arange(gz, dtype=I32)[None, :]
    e_j = jnp.minimum(jnp.sum(strip_end[:, None, :] <= j[:, :, None], axis=2), N_EXPERTS - 1)
    t_ids = jnp.arange(nt, dtype=I32)[:, None]
    y_src = jnp.where(j < strip_end[:, -1:],
                      blk_start[e_j] * bg + run_start[t_ids, e_j] + j - strip_start[t_ids, e_j], 0)
    return blk_e, n_used, z_src.reshape(-1).astype(I32), y_src.reshape(-1).astype(I32)


def kernel(x, ln_in_g, ln_in_b, w_in, b_in, q_norm_g, k_norm_g, sc_conv_w, cf_conv_w, cf_conv_b, cf_ln_g, cf_ln_b, pool_w, pool_scale, w_branch, w_out, ln1_g, ln1_b, w_router, b_router, w_gate_up, b_gate_up, w_down, b_down, ln2_g, ln2_b):
    batch, seq, d = x.shape
    depth = w_in.shape[0]
    n = batch * seq
    alpha = float((2.0 * depth) ** 0.25)
    tile = _tile(seq, MOE_TILE)
    n_blocks = pl.cdiv((n // tile) * _tile_sorted_rows(tile), MOE_BLOCK) + N_EXPERTS

    cos, sin = _rope_tables(seq)
    pl_mask, pl_lo, pl_hi = _pool_constants()
    q_scale = jnp.concatenate([jnp.full((1, 2 * LANES), HEAD_DIM ** -0.5 * np.log2(np.e), F32),
                               jnp.ones((1, LANES), F32)], axis=-1)
    splits = np.cumsum([QKV_COLS, 3 * BRANCH_W, 2 * BRANCH_W, BRANCH_W])

    h, hb = _entry_ln(x.reshape(n, d), ln_in_g, ln_in_b)
    for l in range(depth):
        wl = w_in[l].astype(BF16)
        bl = b_in[l].reshape(1, -1)
        w_parts = [wl[:, :splits[0]]] + [wl[:, splits[i]:splits[i + 1]] for i in range(3)] + [wl[:, splits[3]:]]
        b_parts = [bl[:, :splits[0]]] + [bl[:, splits[i]:splits[i + 1]] for i in range(3)] + [bl[:, splits[3]:]]
        gain = jnp.concatenate([jnp.tile(q_norm_g[l], N_Q_HEADS), jnp.tile(k_norm_g[l], N_KV_HEADS)]).reshape(1, -1)

        q, kt, v = _qkv_proj(hb, w_parts[0], b_parts[0], gain, q_scale, cos, sin, batch, seq)
        zsc, zcf, zpl, zg = _z_proj(hb, w_parts[1:], b_parts[1:])
        o = _attention(q, kt, v, batch, seq)

        r_pad = LANES - N_EXPERTS
        params = dict(
            sc_w=sc_conv_w[l], cf_w=cf_conv_w[l], cf_b=cf_conv_b[l].reshape(1, -1), cf_g=cf_ln_g[l].reshape(1, -1),
            cf_beta=cf_ln_b[l].reshape(1, -1), pl_mask=pl_mask, pl_lo=pl_lo, pl_hi=pl_hi,
            pl_w=_block_diag(pool_w[l]).astype(BF16), pl_s=pool_scale[l].reshape(1, -1),
            w_br=w_branch[l].astype(BF16), w_out=w_out[l].astype(BF16),
            ln1_g=ln1_g[l].reshape(1, -1), ln1_b=ln1_b[l].reshape(1, -1),
            w_r=jnp.pad(w_router[l], ((0, 0), (0, r_pad))).astype(BF16),
            b_r=jnp.pad(b_router[l].astype(F32), (0, r_pad), constant_values=NEG_BIG).reshape(1, -1),
        )
        h1, _, rw, rpos, cnt, zt = _mixer(o, zsc, zcf, zpl, zg, h, params, seq, alpha)

        blk_e, n_used, z_src, y_src = _moe_tables(cnt[:, 0, :N_EXPERTS], tile, n_blocks)
        y = _experts(zt, blk_e, n_used, z_src, n_blocks, l, w_gate_up, b_gate_up[:, :, None, :],
                     w_down, b_down[:, :, None, :])
        h, hb = _combine(y, y_src, rw, rpos, h1, ln2_g[l], ln2_b[l], alpha, tile)
    return h.reshape(batch, seq, d)
```

```python
import functools

import jax
import jax.numpy as jnp
import numpy as np
from jax import lax
from jax.experimental import pallas as pl
from jax.experimental.pallas import tpu as pltpu

F32 = jnp.float32
BF16 = jnp.bfloat16
I32 = jnp.int32

GRID_W = 64
HEAD_DIM = 64
N_Q_HEADS = 4
N_KV_HEADS = 2
Q_GROUP = N_Q_HEADS // N_KV_HEADS
ROPE_HALF = HEAD_DIM // 2
ROPE_THETA = 10000.0
BRANCH_W = 256
N_BRANCHES = 4
SC_WIDTH = 3
CF_WIDTH = 31
POOL_WINDOWS = (2, 4, 8, 16)
POOL_GROUP = BRANCH_W // len(POOL_WINDOWS)
POOL_TAPS = 16
POOL_FIRST = -8
N_EXPERTS = 32
TOP_K = 4
SWIGLU_LIMIT = 7.0
SWIGLU_ALPHA = 1.702
LN_EPS = 1e-5
RMS_EPS = 1e-6
QKV_COLS = N_Q_HEADS * HEAD_DIM + 2 * N_KV_HEADS * HEAD_DIM

LANES = 128
HALO = 16
ROW_GROUP = 8
MOE_TILE = 256
MOE_BLOCK = 512
VMEM_LIMIT = 56 * 1024 * 1024
NEG_BIG = -1e30


def _tile_sorted_rows(t):
    return t * TOP_K + N_EXPERTS * ROW_GROUP


def _cparams(*sem):
    return pltpu.CompilerParams(dimension_semantics=sem, vmem_limit_bytes=VMEM_LIMIT)


def _tile(n, pref):
    t = min(n, pref)
    assert n % t == 0, (n, t)
    return t


def _const_spec(shape):
    nd = len(shape)
    return pl.BlockSpec(shape, lambda *_: (0,) * nd)


def _layer_norm_rows(y, g, b):
    mu = jnp.mean(y, axis=-1, keepdims=True)
    d = y - mu
    var = jnp.mean(d * d, axis=-1, keepdims=True)
    return d * lax.rsqrt(var + LN_EPS) * g + b


def _sigmoid(x):
    return 1.0 / (1.0 + jnp.exp(-x))


def _ln_kernel(x_ref, g_ref, b_ref, h_ref, hb_ref):
    h = _layer_norm_rows(x_ref[...], g_ref[...], b_ref[...])
    h_ref[...] = h
    hb_ref[...] = h.astype(BF16)


def _entry_ln(x2, g, b):
    n, d = x2.shape
    t = _tile(n, 512)
    return pl.pallas_call(
        _ln_kernel,
        grid=(n // t,),
        in_specs=[pl.BlockSpec((t, d), lambda i: (i, 0)), _const_spec((1, d)), _const_spec((1, d))],
        out_specs=[pl.BlockSpec((t, d), lambda i: (i, 0)), pl.BlockSpec((t, d), lambda i: (i, 0))],
        out_shape=[jax.ShapeDtypeStruct((n, d), F32), jax.ShapeDtypeStruct((n, d), BF16)],
        compiler_params=_cparams("parallel"),
        name="entry_ln",
    )(x2, g.reshape(1, d), b.reshape(1, d))


def _qkv_kernel(hb_ref, w_ref, b_ref, gain_ref, scale_ref, cos_ref, sin_ref, q_ref, kt_ref, v_ref):
    z = jnp.dot(hb_ref[...], w_ref[...], preferred_element_type=F32) + b_ref[...]
    t = z.shape[0]
    lane = lax.broadcasted_iota(I32, (t, LANES), 1)
    low_head = lane < HEAD_DIM
    first_half = (lane % ROPE_HALF) < (ROPE_HALF // 2)
    cos = cos_ref[...]
    sin = sin_ref[...]
    outs = []
    for c in range(3):
        x = z[:, c * LANES:(c + 1) * LANES]
        x2 = x * x
        s_all = jnp.sum(x2, axis=-1, keepdims=True)
        s_low = jnp.sum(jnp.where(low_head, x2, 0.0), axis=-1, keepdims=True)
        ms = jnp.where(low_head, s_low, s_all - s_low) * (1.0 / HEAD_DIM)
        xn = x * lax.rsqrt(ms + RMS_EPS) * gain_ref[:, c * LANES:(c + 1) * LANES]
        partner = jnp.where(first_half, pltpu.roll(xn, LANES - ROPE_HALF // 2, 1), pltpu.roll(xn, ROPE_HALF // 2, 1))
        outs.append((xn * cos + partner * sin) * scale_ref[:, c * LANES:(c + 1) * LANES])
    q_ref[:, 0:LANES] = outs[0].astype(BF16)
    q_ref[:, LANES:2 * LANES] = outs[1].astype(BF16)
    kt_ref[...] = outs[2].T.astype(BF16)
    v = z[:, 3 * LANES:4 * LANES]
    tail = jnp.where(lane == HEAD_DIM, 1.0, 0.0)
    v_ref[0] = jnp.where(low_head, v, tail).astype(BF16)
    v_ref[1] = jnp.where(low_head, pltpu.roll(v, HEAD_DIM, 1), tail).astype(BF16)


def _attn_tiles(seq):
    return _tile(seq, 512), _tile(seq, 2048)


def _qkv_proj(hb, w, b, gain, scale, cos, sin, batch, seq):
    n, d = hb.shape
    t = _tile(seq, 512)
    tk = _attn_tiles(seq)[1]
    assert tk % t == 0
    per_seq, per_chunk, nc = seq // t, tk // t, seq // tk
    row = lambda i: (i, 0)
    return pl.pallas_call(
        _qkv_kernel,
        grid=(n // t,),
        in_specs=[pl.BlockSpec((t, d), row), _const_spec((d, QKV_COLS)), _const_spec((1, QKV_COLS)),
                  _const_spec((1, 3 * LANES)), _const_spec((1, 3 * LANES)),
                  pl.BlockSpec((t, LANES), lambda i: (i % per_seq, 0)),
                  pl.BlockSpec((t, LANES), lambda i: (i % per_seq, 0))],
        out_specs=[pl.BlockSpec((t, 2 * LANES), row),
                   pl.BlockSpec((None, None, LANES, t),
                                lambda i: (i // per_seq, (i % per_seq) // per_chunk, 0, i % per_chunk)),
                   pl.BlockSpec((None, None, N_KV_HEADS, t, LANES),
                                lambda i: (i // per_seq, (i % per_seq) // per_chunk, 0, i % per_chunk, 0))],
        out_shape=[jax.ShapeDtypeStruct((n, 2 * LANES), BF16),
                   jax.ShapeDtypeStruct((batch, nc, LANES, tk), BF16),
                   jax.ShapeDtypeStruct((batch, nc, N_KV_HEADS, tk, LANES), BF16)],
        compiler_params=_cparams("parallel"),
        name="qkv_proj",
    )(hb, w, b, gain, scale, cos, sin)


def _zproj_kernel(hb_ref, wsc_ref, wcf_ref, wpl_ref, wg_ref, bsc_ref, bcf_ref, bpl_ref, bg_ref,
                  zsc_ref, zcf_ref, zpl_ref, zg_ref):
    x = hb_ref[...]
    for w_ref, b_ref, o_ref in ((wsc_ref, bsc_ref, zsc_ref), (wcf_ref, bcf_ref, zcf_ref),
                                (wpl_ref, bpl_ref, zpl_ref), (wg_ref, bg_ref, zg_ref)):
        o_ref[...] = (jnp.dot(x, w_ref[...], preferred_element_type=F32) + b_ref[...]).astype(BF16)


def _z_proj(hb, ws, bs):
    n, d = hb.shape
    t = _tile(n, 256)
    row = lambda i: (i, 0)
    widths = [w.shape[1] for w in ws]
    return pl.pallas_call(
        _zproj_kernel,
        grid=(n // t,),
        in_specs=[pl.BlockSpec((t, d), row)]
        + [pl.BlockSpec((d, c), lambda i: (0, 0), pipeline_mode=pl.Buffered(1)) for c in widths]
        + [_const_spec((1, c)) for c in widths],
        out_specs=[pl.BlockSpec((t, c), row) for c in widths],
        out_shape=[jax.ShapeDtypeStruct((n, c), BF16) for c in widths],
        compiler_params=_cparams("parallel"),
        name="z_proj",
    )(hb, *ws, *bs)


def _attn_kernel(q_ref, kt_ref, v_ref, o_ref, m_scr, acc_scr):
    h = pl.program_id(1)
    tq = q_ref.shape[0]
    rows = Q_GROUP * tq
    n_chunks, _, tk = kt_ref.shape
    lane = lax.broadcasted_iota(I32, (tq, LANES), 1)
    own_half = (lane // HEAD_DIM) == h
    first = h == 0
    qf = q_ref[...].astype(F32)
    swapped = pltpu.roll(qf, HEAD_DIM, 1)
    q = jnp.concatenate([jnp.where(own_half, jnp.where(first, qf, swapped), 0.0),
                         jnp.where(own_half, jnp.where(first, swapped, qf), 0.0)], axis=0).astype(BF16)
    m_scr[...] = jnp.full(m_scr.shape, -jnp.inf, F32)
    acc_scr[...] = jnp.zeros(acc_scr.shape, F32)

    def body(c, carry):
        s = jnp.dot(q, kt_ref[c], preferred_element_type=F32)
        m_prev = m_scr[...]
        m_new = jnp.maximum(m_prev, jnp.max(s, axis=-1, keepdims=True))
        alpha = jnp.exp2(m_prev - m_new)
        p = jnp.exp2(s - jnp.concatenate([m_new] * (tk // LANES), axis=1)).astype(BF16)
        acc_scr[...] = alpha * acc_scr[...] + jnp.dot(p, v_ref[c], preferred_element_type=F32)
        m_scr[...] = m_new
        return carry

    lax.fori_loop(0, n_chunks, body, 0, unroll=2 if n_chunks % 2 == 0 else 1)
    acc = acc_scr[...]
    res = acc / acc[:, HEAD_DIM:HEAD_DIM + 1]
    o_ref[...] = jnp.where(lane < HEAD_DIM, res[:tq], pltpu.roll(res[tq:], HEAD_DIM, 1)).astype(BF16)


def _attention(q, kt, v, batch, seq):
    tq, tk = _attn_tiles(seq)
    nc = seq // tk
    per_seq = seq // tq
    rows = Q_GROUP * tq
    return pl.pallas_call(
        _attn_kernel,
        grid=(batch, N_KV_HEADS, per_seq),
        in_specs=[pl.BlockSpec((tq, LANES), lambda b, h, i: (b * per_seq + i, h)),
                  pl.BlockSpec((None, nc, LANES, tk), lambda b, h, i: (b, 0, 0, 0)),
                  pl.BlockSpec((None, nc, None, tk, LANES), lambda b, h, i: (b, 0, h, 0, 0))],
        out_specs=pl.BlockSpec((None, tq, LANES), lambda b, h, i: (b, i, h)),
        out_shape=jax.ShapeDtypeStruct((batch, seq, N_KV_HEADS * LANES), BF16),
        scratch_shapes=[pltpu.VMEM((rows, LANES), F32), pltpu.VMEM((rows, LANES), F32)],
        compiler_params=_cparams("parallel", "parallel", "arbitrary"),
        name="attention",
    )(q, kt, v).reshape(batch * seq, N_KV_HEADS * LANES)


def _dwconv(xs_ref, w_ref, n_taps, first_row, rows, shifted_ref=None):
    if shifted_ref is None:
        taps = [xs_ref[pl.ds(first_row + j, rows), :] for j in range(n_taps)]
    else:
        span = shifted_ref.shape[1]
        assert first_row + n_taps - 1 + rows <= span + ROW_GROUP - 1
        shifts = sorted({(first_row + j) % ROW_GROUP for j in range(n_taps)} - {0})
        for s in shifts:
            shifted_ref[s] = xs_ref[pl.ds(s, span), :]
        taps = []
        for j in range(n_taps):
            s, base = (first_row + j) % ROW_GROUP, (first_row + j) // ROW_GROUP * ROW_GROUP
            src = xs_ref if s == 0 else shifted_ref.at[s]
            taps.append(src[pl.ds(base, rows), :])
    acc = None
    for j, tap in enumerate(taps):
        term = tap * w_ref[j:j + 1, :]
        acc = term if acc is None else acc + term
    return acc


def _mixer_kernel(seq, alpha,
                  o_ref, scp_ref, scc_ref, scn_ref, cfp_ref, cfc_ref, cfn_ref, plp_ref, plc_ref, pln_ref,
                  zg_ref, h_ref, scw_ref, cfw_ref, cfb_ref, cfg_ref, cfbeta_ref, plmask_ref, pllo_ref, plhi_ref,
                  plw_ref, pls_ref, wbr_ref, wout_ref, ln1g_ref, ln1b_ref, wr_ref, br_ref,
                  h1_ref, ridx_ref, rw_ref, rpos_ref, cnt_ref, zt_ref,
                  xsc, xcf, xpl, xsh):
    t = h_ref.shape[0]
    w = BRANCH_W
    i = pl.program_id(0)
    per_seq = seq // t
    j = i % per_seq
    has_prev = (j > 0).astype(F32)
    has_next = (j < per_seq - 1).astype(F32)

    def sc_v(z):
        z = z.astype(F32)
        return z[:, w:2 * w] * z[:, 2 * w:3 * w]
    xsc[0:HALO, :] = sc_v(scp_ref[...]) * has_prev
    xsc[HALO:HALO + t, :] = sc_v(scc_ref[...])
    xsc[HALO + t:2 * HALO + t, :] = sc_v(scn_ref[...]) * has_next
    y_sc = scc_ref[:, 0:w].astype(F32) * _dwconv(xsc, scw_ref, SC_WIDTH, HALO - SC_WIDTH // 2, t)

    def cf_v(z):
        z = z.astype(F32)
        return z[:, 0:w] * _sigmoid(z[:, w:2 * w])
    xcf[0:HALO, :] = cf_v(cfp_ref[...]) * has_prev
    xcf[HALO:HALO + t, :] = cf_v(cfc_ref[...])
    xcf[HALO + t:2 * HALO + t, :] = cf_v(cfn_ref[...]) * has_next
    y_cf = _dwconv(xcf, cfw_ref, CF_WIDTH, HALO - CF_WIDTH // 2, t, xsh) + cfb_ref[...]
    y_cf = _layer_norm_rows(y_cf, cfg_ref[...], cfbeta_ref[...])
    y_cf = y_cf * _sigmoid(y_cf)

    u = plc_ref[...].astype(F32)
    xpl[0:HALO, :] = plp_ref[...].astype(F32) * has_prev
    xpl[HALO:HALO + t, :] = u
    xpl[HALO + t:2 * HALO + t, :] = pln_ref[...].astype(F32) * has_next
    win_sum = _dwconv(xpl, plmask_ref, POOL_TAPS, HALO + POOL_FIRST, t, xsh)
    pos = (j * t + lax.broadcasted_iota(I32, (t, w), 0)).astype(F32)
    n_win = jnp.minimum(pos + plhi_ref[...], float(seq - 1)) - jnp.maximum(pos - pllo_ref[...], 0.0) + 1.0
    y_pl = win_sum / n_win - u
    y_pl = jnp.dot(y_pl.astype(BF16), plw_ref[...], preferred_element_type=F32) * pls_ref[...]

    merged = None
    for g, br in enumerate((o_ref[...], y_sc.astype(BF16), y_cf.astype(BF16), y_pl.astype(BF16))):
        proj = jnp.dot(br, wbr_ref[g], preferred_element_type=F32)
        gate = 1.0 / (1.0 + jnp.exp2(zg_ref[:, g * proj.shape[1]:(g + 1) * proj.shape[1]].astype(F32)))
        merged = gate * proj if merged is None else merged + gate * proj
    mix = jnp.dot(merged.astype(BF16), wout_ref[...], preferred_element_type=F32)
    h1 = _layer_norm_rows(alpha * h_ref[...] + mix, ln1g_ref[...], ln1b_ref[...])
    h1_ref[...] = h1

    logits = jnp.dot(h1.astype(BF16), wr_ref[...], preferred_element_type=F32) + br_ref[...]
    lane = lax.broadcasted_iota(I32, (t, LANES), 1)
    vals = logits
    tops, sels, idxs = [], [], []
    for _ in range(TOP_K):
        m = jnp.max(vals, axis=-1, keepdims=True)
        first = jnp.min(jnp.where(vals == m, lane, LANES), axis=-1, keepdims=True)
        sel = lane == first
        vals = jnp.where(sel, -jnp.inf, vals)
        tops.append(m)
        sels.append(sel)
        idxs.append(first)
    exps = [jnp.exp(m - tops[0]) for m in tops]
    denom = exps[0] + exps[1] + exps[2] + exps[3]

    onehot = jnp.where(sels[0] | sels[1] | sels[2] | sels[3], 1.0, 0.0)
    r_id = lax.broadcasted_iota(I32, (t, t), 0)
    c_id = lax.broadcasted_iota(I32, (t, t), 1)
    lower = jnp.where(c_id < r_id, 1.0, 0.0).astype(BF16)
    rank_e = jnp.dot(lower, onehot.astype(BF16), preferred_element_type=F32)
    cnt = jnp.sum(onehot, axis=0, keepdims=True)
    groups = jnp.floor((cnt + (ROW_GROUP - 1.0)) * (1.0 / ROW_GROUP))
    e_r = lax.broadcasted_iota(I32, (LANES, LANES), 0)
    e_c = lax.broadcasted_iota(I32, (LANES, LANES), 1)
    before = jnp.where(e_r < e_c, 1.0, 0.0).astype(BF16)
    strip_start = jnp.dot(jnp.broadcast_to(groups, (ROW_GROUP, LANES)).astype(BF16), before,
                          preferred_element_type=F32)[0:1, :]
    row_e = rank_e + ROW_GROUP * strip_start
    cnt_ref[...] = jnp.broadcast_to(cnt, cnt_ref.shape).astype(I32)

    zr = zt_ref.shape[0]
    z_row = lax.broadcasted_iota(I32, (t, zr), 1)
    ridx = jnp.zeros((t, LANES), I32)
    rw = jnp.zeros((t, LANES), F32)
    rpos = jnp.zeros((t, LANES), I32)
    scatter = jnp.zeros((t, zr), F32)
    for k in range(TOP_K):
        pos_k = jnp.sum(jnp.where(sels[k], row_e, 0.0), axis=-1, keepdims=True).astype(I32)
        scatter = jnp.where(z_row == pos_k, 1.0, scatter)
        ridx = jnp.where(lane == k, idxs[k], ridx)
        rw = jnp.where(lane == k, exps[k] / denom, rw)
        rpos = jnp.where(lane == k, pos_k, rpos)
    ridx_ref[...] = ridx
    rw_ref[...] = rw
    rpos_ref[...] = rpos
    zt_ref[...] = lax.dot_general(scatter.astype(BF16), h1.astype(BF16), (((0,), (0,)), ((), ())),
                                  preferred_element_type=F32)


def _mixer(o, zsc, zcf, zpl, zg, h, p, seq, alpha):
    n, d = h.shape
    t = _tile(seq, MOE_TILE)
    hb = t // HALO
    n_halo = n // HALO
    row = lambda i: (i, 0)
    prev = lambda i: (jnp.maximum(i * hb - 1, 0), 0)
    nxt = lambda i: (jnp.minimum((i + 1) * hb, n_halo - 1), 0)

    def halo_specs(c):
        return [pl.BlockSpec((HALO, c), prev), pl.BlockSpec((t, c), row), pl.BlockSpec((HALO, c), nxt)]

    consts = [p["sc_w"], p["cf_w"], p["cf_b"], p["cf_g"], p["cf_beta"], p["pl_mask"], p["pl_lo"], p["pl_hi"],
              p["pl_w"], p["pl_s"], p["w_br"], p["w_out"], p["ln1_g"], p["ln1_b"], p["w_r"], p["b_r"]]
    lane_out = lambda dt: jax.ShapeDtypeStruct((n, LANES), dt)
    n_tiles = n // t
    zr = _tile_sorted_rows(t)
    return pl.pallas_call(
        functools.partial(_mixer_kernel, seq, alpha),
        grid=(n_tiles,),
        in_specs=[pl.BlockSpec((t, BRANCH_W), row)] + halo_specs(3 * BRANCH_W) + halo_specs(2 * BRANCH_W)
        + halo_specs(BRANCH_W) + [pl.BlockSpec((t, N_BRANCHES * d), row), pl.BlockSpec((t, d), row)]
        + [_const_spec(c.shape) for c in consts],
        out_specs=[pl.BlockSpec((t, d), row), pl.BlockSpec((t, LANES), row), pl.BlockSpec((t, LANES), row),
                   pl.BlockSpec((t, LANES), row), pl.BlockSpec((None, ROW_GROUP, LANES), lambda i: (i, 0, 0)),
                   pl.BlockSpec((zr, d), row)],
        out_shape=[jax.ShapeDtypeStruct((n, d), F32), lane_out(I32), lane_out(F32), lane_out(I32),
                   jax.ShapeDtypeStruct((n_tiles, ROW_GROUP, LANES), I32),
                   jax.ShapeDtypeStruct((n_tiles * zr, d), F32)],
        scratch_shapes=[pltpu.VMEM((t + 2 * HALO, BRANCH_W), F32)] * 3
        + [pltpu.VMEM((ROW_GROUP, t + 2 * HALO - ROW_GROUP, BRANCH_W), F32)],
        compiler_params=_cparams("parallel"),
        name="mixer",
    )(o, zsc, zsc, zsc, zcf, zcf, zcf, zpl, zpl, zpl, zg, h, *consts)


def _group_copy(src_hbm, group, dst, dst_group, sem):
    return pltpu.make_async_copy(src_hbm.at[pl.ds(pl.multiple_of(group * ROW_GROUP, ROW_GROUP), ROW_GROUP), :],
                                 dst.at[pl.ds(dst_group * ROW_GROUP, ROW_GROUP), :], sem)


def _cast_rows(src_ref, dst_ref, chunk):
    def body(r, c):
        rows = pl.ds(pl.multiple_of(r * chunk, chunk), chunk)
        dst_ref[rows, :] = src_ref[rows, :].astype(dst_ref.dtype)
        return c
    lax.fori_loop(0, src_ref.shape[0] // chunk, body, 0)


def _expert_kernel(blk_e_ref, n_used_ref, src_ref, zt_hbm, wgu32_ref, bgu_ref, wd32_ref, bd_ref, y_ref,
                   xbuf, wgu_ref, wd_ref, sem):
    b = pl.program_id(0)
    n_used = n_used_ref[0]
    groups = MOE_BLOCK // ROW_GROUP

    @pl.when((b < n_used) & ((b == 0) | (blk_e_ref[b] != blk_e_ref[jnp.maximum(b - 1, 0)])))
    def _():
        _cast_rows(wgu32_ref, wgu_ref, LANES)
        _cast_rows(wd32_ref, wd_ref, LANES)

    def gather(block, slot):
        for g in range(groups):
            _group_copy(zt_hbm, src_ref[block * groups + g], xbuf.at[slot], g, sem.at[slot]).start()

    @pl.when(b == 0)
    def _():
        gather(0, 0)

    @pl.when(b + 1 < n_used)
    def _():
        gather(b + 1, (b + 1) % 2)

    @pl.when(b < n_used)
    def _():
        slot = b % 2
        pltpu.make_async_copy(xbuf.at[slot], xbuf.at[slot], sem.at[slot]).wait()
        f = wd_ref.shape[0]
        x = xbuf[slot].astype(BF16)
        hgu = jnp.dot(x, wgu_ref[...], preferred_element_type=F32) + bgu_ref[...]
        glu = jnp.minimum(hgu[:, :f], SWIGLU_LIMIT)
        lin = jnp.clip(hgu[:, f:], -SWIGLU_LIMIT, SWIGLU_LIMIT)
        act = glu * _sigmoid(SWIGLU_ALPHA * glu) * (lin + 1.0)
        y_ref[...] = jnp.dot(act.astype(BF16), wd_ref[...], preferred_element_type=F32) + bd_ref[...]

    @pl.when(b >= n_used)
    def _():
        y_ref[...] = jnp.zeros(y_ref.shape, F32)


def _experts(zt, blk_e, n_used, src_groups, n_blocks, layer, wgu, bgu, wd, bd):
    d = zt.shape[1]
    f2 = wgu.shape[3]
    expert = lambda b, be, nu, sg: (layer, be[b], 0, 0)
    return pl.pallas_call(
        _expert_kernel,
        grid_spec=pltpu.PrefetchScalarGridSpec(
            num_scalar_prefetch=3,
            grid=(n_blocks,),
            in_specs=[pl.BlockSpec(memory_space=pl.ANY),
                      pl.BlockSpec((None, None, d, f2), expert), pl.BlockSpec((None, None, 1, f2), expert),
                      pl.BlockSpec((None, None, f2 // 2, d), expert), pl.BlockSpec((None, None, 1, d), expert)],
            out_specs=pl.BlockSpec((MOE_BLOCK, d), lambda b, be, nu, sg: (b, 0)),
            scratch_shapes=[pltpu.VMEM((2, MOE_BLOCK, d), F32), pltpu.VMEM((d, f2), BF16),
                            pltpu.VMEM((f2 // 2, d), BF16), pltpu.SemaphoreType.DMA((2,))],
        ),
        out_shape=jax.ShapeDtypeStruct((n_blocks * MOE_BLOCK, d), F32),
        compiler_params=_cparams("arbitrary"),
        name="moe_experts",
    )(blk_e, n_used, src_groups, zt, wgu, bgu, wd, bd)


def _combine_kernel(alpha, src_ref, y_hbm, rw_ref, rpos_ref, h1_ref, g_ref, b_ref, h_ref, hb_ref, buf, sem):
    i = pl.program_id(0)
    n_steps = pl.num_programs(0)
    zr = buf.shape[1]
    groups = zr // ROW_GROUP
    t = h1_ref.shape[0]

    def gather(tile, slot):
        for g in range(groups):
            _group_copy(y_hbm, src_ref[tile * groups + g], buf.at[slot], g, sem.at[slot]).start(priority=g % 2)

    @pl.when(i == 0)
    def _():
        gather(0, 0)

    @pl.when(i + 1 < n_steps)
    def _():
        gather(i + 1, (i + 1) % 2)

    slot = i % 2
    pltpu.make_async_copy(buf.at[slot], buf.at[slot], sem.at[slot]).wait()

    lane = lax.broadcasted_iota(I32, (t, LANES), 1)
    z_row = lax.broadcasted_iota(I32, (t, zr), 1)
    rw = rw_ref[...]
    rpos = rpos_ref[...]
    weights = jnp.zeros((t, zr), F32)
    for k in range(TOP_K):
        w_k = jnp.sum(jnp.where(lane == k, rw, 0.0), axis=-1, keepdims=True)
        pos_k = jnp.sum(jnp.where(lane == k, rpos, 0), axis=-1, keepdims=True)
        weights = jnp.where(z_row == pos_k, w_k, weights)
    ffn = jnp.dot(weights.astype(BF16), buf[slot].astype(BF16), preferred_element_type=F32)
    h = _layer_norm_rows(alpha * h1_ref[...] + ffn, g_ref[...], b_ref[...])
    h_ref[...] = h
    hb_ref[...] = h.astype(BF16)


def _combine(y, src_groups, rw, rpos, h1, g, b, alpha, tile):
    n, d = h1.shape
    zr = _tile_sorted_rows(tile)
    row = lambda i, sg: (i, 0)
    const = lambda i, sg: (0, 0)
    return pl.pallas_call(
        functools.partial(_combine_kernel, alpha),
        grid_spec=pltpu.PrefetchScalarGridSpec(
            num_scalar_prefetch=1,
            grid=(n // tile,),
            in_specs=[pl.BlockSpec(memory_space=pl.ANY), pl.BlockSpec((tile, LANES), row),
                      pl.BlockSpec((tile, LANES), row), pl.BlockSpec((tile, d), row),
                      pl.BlockSpec((1, d), const), pl.BlockSpec((1, d), const)],
            out_specs=[pl.BlockSpec((tile, d), row), pl.BlockSpec((tile, d), row)],
            scratch_shapes=[pltpu.VMEM((2, zr, d), F32), pltpu.SemaphoreType.DMA((2,))],
        ),
        out_shape=[jax.ShapeDtypeStruct((n, d), F32), jax.ShapeDtypeStruct((n, d), BF16)],
        compiler_params=_cparams("arbitrary"),
        name="moe_combine",
    )(src_groups, y, rw, rpos, h1, g.reshape(1, d), b.reshape(1, d))


def _rope_tables(seq):
    rows = seq // GRID_W
    row = jnp.repeat(jnp.arange(rows, dtype=F32), GRID_W)
    col = jnp.tile(jnp.arange(GRID_W, dtype=F32), rows)
    inv = ROPE_THETA ** (-jnp.arange(0, ROPE_HALF, 2, dtype=F32) / ROPE_HALF)
    ang_r = row[:, None] * inv
    ang_c = col[:, None] * inv
    cr, sr, cc, sc = jnp.cos(ang_r), jnp.sin(ang_r), jnp.cos(ang_c), jnp.sin(ang_c)
    cos = jnp.concatenate([cr, cr, cc, cc], axis=-1)
    sin = jnp.concatenate([-sr, sr, -sc, sc], axis=-1)
    return jnp.tile(cos, (1, 2)), jnp.tile(sin, (1, 2))


def _pool_constants():
    offs = np.arange(POOL_TAPS) + POOL_FIRST
    mask = np.zeros((POOL_TAPS, BRANCH_W), np.float32)
    lo = np.zeros((1, BRANCH_W), np.float32)
    hi = np.zeros((1, BRANCH_W), np.float32)
    for g, win in enumerate(POOL_WINDOWS):
        sl = slice(g * POOL_GROUP, (g + 1) * POOL_GROUP)
        mask[(offs >= -(win // 2)) & (offs <= win - 1 - win // 2), sl] = 1.0
        lo[0, sl] = win // 2
        hi[0, sl] = win - 1 - win // 2
    return jnp.asarray(mask), jnp.asarray(lo), jnp.asarray(hi)


def _block_diag(blocks):
    g, a, b = blocks.shape
    out = jnp.zeros((g * a, g * b), blocks.dtype)
    for i in range(g):
        out = out.at[i * a:(i + 1) * a, i * b:(i + 1) * b].set(blocks[i])
    return out


def _moe_tables(tile_counts, tile, n_blocks):
    nt = tile_counts.shape[0]
    bg = MOE_BLOCK // ROW_GROUP
    gz = _tile_sorted_rows(tile) // ROW_GROUP
    ng = (tile_counts + ROW_GROUP - 1) // ROW_GROUP
    strip_end = jnp.cumsum(ng, axis=1)
    strip_start = strip_end - ng
    run_end = jnp.cumsum(ng, axis=0)
    run_start = run_end - ng
    total = run_end[-1]
    blocks_e = (total + bg - 1) // bg
    blk_end = jnp.cumsum(blocks_e)
    blk_start = blk_end - blocks_e
    n_used = blk_end[-1:].astype(I32)
    b_ids = jnp.arange(n_blocks, dtype=I32)
    blk_e = jnp.minimum(jnp.sum(blk_end[None, :] <= b_ids[:, None], axis=1), N_EXPERTS - 1).astype(I32)

    experts = jnp.arange(N_EXPERTS, dtype=I32)
    is_e = blk_e[:, None] == experts[None, :]
    pick = lambda tab: jnp.sum(jnp.where(is_e, tab[None, :], 0), axis=1)
    columns = lambda tab: jnp.dot(is_e.astype(F32), tab.T.astype(F32),
                                  precision=lax.Precision.HIGHEST).astype(I32)
    q = (b_ids - pick(blk_start))[:, None] * bg + jnp.arange(bg, dtype=I32)[None, :]
    starts, ends = columns(run_start), columns(run_end)
    offset = columns(jnp.arange(nt, dtype=I32)[:, None] * gz + strip_start - run_start)
    in_strip = (starts[:, None, :] <= q[:, :, None]) & (q[:, :, None] < ends[:, None, :])
    z_src = jnp.where(q < pick(total)[:, None],
                      jnp.sum(jnp.where(in_strip, offset[:, None, :], 0), axis=2) + q, 0)

    j = jnp.arange(gz, dtype=I32)[None, :]
    e_j = jnp.minimum(jnp.sum(strip_end[:, None, :] <= j[:, :, None], axis=2), N_EXPERTS - 1)
    base = blk_start[None, :] * bg + run_start - strip_start
    picked = jnp.sum(jnp.where(e_j[:, :, None] == experts[None, None, :], base[:, None, :], 0), axis=2)
    y_src = jnp.where(j < strip_end[:, -1:], picked + j, 0)
    return blk_e, n_used, z_src.reshape(-1).astype(I32), y_src.reshape(-1).astype(I32)


def kernel(x, ln_in_g, ln_in_b, w_in, b_in, q_norm_g, k_norm_g, sc_conv_w, cf_conv_w, cf_conv_b, cf_ln_g, cf_ln_b, pool_w, pool_scale, w_branch, w_out, ln1_g, ln1_b, w_router, b_router, w_gate_up, b_gate_up, w_down, b_down, ln2_g, ln2_b):
    batch, seq, d = x.shape
    depth = w_in.shape[0]
    n = batch * seq
    alpha = float((2.0 * depth) ** 0.25)
    tile = _tile(seq, MOE_TILE)
    n_blocks = pl.cdiv((n // tile) * _tile_sorted_rows(tile), MOE_BLOCK) + N_EXPERTS

    cos, sin = _rope_tables(seq)
    pl_mask, pl_lo, pl_hi = _pool_constants()
    q_scale = jnp.concatenate([jnp.full((1, 2 * LANES), HEAD_DIM ** -0.5 * np.log2(np.e), F32),
                               jnp.ones((1, LANES), F32)], axis=-1)
    splits = np.cumsum([QKV_COLS, 3 * BRANCH_W, 2 * BRANCH_W, BRANCH_W])

    h, hb = _entry_ln(x.reshape(n, d), ln_in_g, ln_in_b)
    for l in range(depth):
        col_scale = jnp.concatenate([jnp.ones((splits[3],), F32), jnp.full((w_in.shape[2] - splits[3],), -np.log2(np.e), F32)])
        wl = (w_in[l] * col_scale).astype(BF16)
        bl = (b_in[l] * col_scale).reshape(1, -1)
        w_parts = [wl[:, :splits[0]]] + [wl[:, splits[i]:splits[i + 1]] for i in range(3)] + [wl[:, splits[3]:]]
        b_parts = [bl[:, :splits[0]]] + [bl[:, splits[i]:splits[i + 1]] for i in range(3)] + [bl[:, splits[3]:]]
        gain = jnp.concatenate([jnp.tile(q_norm_g[l], N_Q_HEADS), jnp.tile(k_norm_g[l], N_KV_HEADS)]).reshape(1, -1)

        q, kt, v = _qkv_proj(hb, w_parts[0], b_parts[0], gain, q_scale, cos, sin, batch, seq)
        zsc, zcf, zpl, zg = _z_proj(hb, w_parts[1:], b_parts[1:])
        o = _attention(q, kt, v, batch, seq)

        r_pad = LANES - N_EXPERTS
        params = dict(
            sc_w=sc_conv_w[l], cf_w=cf_conv_w[l], cf_b=cf_conv_b[l].reshape(1, -1), cf_g=cf_ln_g[l].reshape(1, -1),
            cf_beta=cf_ln_b[l].reshape(1, -1), pl_mask=pl_mask, pl_lo=pl_lo, pl_hi=pl_hi,
            pl_w=_block_diag(pool_w[l]).astype(BF16), pl_s=pool_scale[l].reshape(1, -1),
            w_br=w_branch[l].astype(BF16), w_out=w_out[l].astype(BF16),
            ln1_g=ln1_g[l].reshape(1, -1), ln1_b=ln1_b[l].reshape(1, -1),
            w_r=jnp.pad(w_router[l], ((0, 0), (0, r_pad))).astype(BF16),
            b_r=jnp.pad(b_router[l].astype(F32), (0, r_pad), constant_values=NEG_BIG).reshape(1, -1),
        )
        h1, _, rw, rpos, cnt, zt = _mixer(o, zsc, zcf, zpl, zg, h, params, seq, alpha)

        blk_e, n_used, z_src, y_src = _moe_tables(cnt[:, 0, :N_EXPERTS], tile, n_blocks)
        y = _experts(zt, blk_e, n_used, z_src, n_blocks, l, w_gate_up, b_gate_up[:, :, None, :],
                     w_down, b_down[:, :, None, :])
        h, hb = _combine(y, y_src, rw, rpos, h1, ln2_g[l], ln2_b[l], alpha, tile)
    return h.reshape(batch, seq, d)
```

```python
import functools

import jax
import jax.numpy as jnp
import numpy as np
from jax import lax
from jax.experimental import pallas as pl
from jax.experimental.pallas import tpu as pltpu

F32 = jnp.float32
BF16 = jnp.bfloat16
I32 = jnp.int32

GRID_W = 64
HEAD_DIM = 64
N_Q_HEADS = 4
N_KV_HEADS = 2
Q_GROUP = N_Q_HEADS // N_KV_HEADS
ROPE_HALF = HEAD_DIM // 2
ROPE_THETA = 10000.0
BRANCH_W = 256
N_BRANCHES = 4
SC_WIDTH = 3
CF_WIDTH = 31
POOL_WINDOWS = (2, 4, 8, 16)
POOL_GROUP = BRANCH_W // len(POOL_WINDOWS)
POOL_TAPS = 16
POOL_FIRST = -8
N_EXPERTS = 32
TOP_K = 4
SWIGLU_LIMIT = 7.0
SWIGLU_ALPHA = 1.702
LN_EPS = 1e-5
RMS_EPS = 1e-6
QKV_COLS = N_Q_HEADS * HEAD_DIM + 2 * N_KV_HEADS * HEAD_DIM

LANES = 128
HALO = 16
ROW_GROUP = 8
MOE_TILE = 256
MOE_BLOCK = 512
PREFETCH_DEPTH = 3
VMEM_LIMIT = 56 * 1024 * 1024
NEG_BIG = -1e30


def _tile_sorted_rows(t):
    return t * TOP_K + N_EXPERTS * ROW_GROUP


def _cparams(*sem):
    return pltpu.CompilerParams(dimension_semantics=sem, vmem_limit_bytes=VMEM_LIMIT)


def _tile(n, pref):
    t = min(n, pref)
    assert n % t == 0, (n, t)
    return t


def _const_spec(shape):
    nd = len(shape)
    return pl.BlockSpec(shape, lambda *_: (0,) * nd)


def _layer_norm_rows(y, g, b):
    mu = jnp.mean(y, axis=-1, keepdims=True)
    d = y - mu
    var = jnp.mean(d * d, axis=-1, keepdims=True)
    return d * lax.rsqrt(var + LN_EPS) * g + b


def _sigmoid(x):
    return 1.0 / (1.0 + jnp.exp(-x))


def _attn_tiles(seq):
    return _tile(seq, 512), _tile(seq, 2048)


def _proj_kernel(apply_ln, *refs):
    refs = list(refs)
    x_ref = refs.pop(0)
    ln_refs = [refs.pop(0), refs.pop(0)] if apply_ln else None
    (wqkv_ref, bqkv_ref, gain_ref, scale_ref, cos_ref, sin_ref,
     wsc_ref, wcf_ref, wpl_ref, wg_ref, bsc_ref, bcf_ref, bpl_ref, bg_ref) = refs[:14]
    outs = refs[14:]
    h = x_ref[...]
    if apply_ln:
        h = _layer_norm_rows(h, ln_refs[0][...], ln_refs[1][...])
        outs.pop(0)[...] = h
    q_ref, kt_ref, v_ref, zsc_ref, zcf_ref, zpl_ref, zg_ref = outs
    hb = h.astype(BF16)

    z = jnp.dot(hb, wqkv_ref[...], preferred_element_type=F32) + bqkv_ref[...]
    t = z.shape[0]
    lane = lax.broadcasted_iota(I32, (t, LANES), 1)
    low_head = lane < HEAD_DIM
    first_half = (lane % ROPE_HALF) < (ROPE_HALF // 2)
    cos = cos_ref[...]
    sin = sin_ref[...]
    roped = []
    for c in range(3):
        x = z[:, c * LANES:(c + 1) * LANES]
        x2 = x * x
        s_all = jnp.sum(x2, axis=-1, keepdims=True)
        s_low = jnp.sum(jnp.where(low_head, x2, 0.0), axis=-1, keepdims=True)
        ms = jnp.where(low_head, s_low, s_all - s_low) * (1.0 / HEAD_DIM)
        xn = x * lax.rsqrt(ms + RMS_EPS) * gain_ref[:, c * LANES:(c + 1) * LANES]
        partner = jnp.where(first_half, pltpu.roll(xn, LANES - ROPE_HALF // 2, 1), pltpu.roll(xn, ROPE_HALF // 2, 1))
        roped.append((xn * cos + partner * sin) * scale_ref[:, c * LANES:(c + 1) * LANES])
    q_ref[:, 0:LANES] = roped[0].astype(BF16)
    q_ref[:, LANES:2 * LANES] = roped[1].astype(BF16)
    kt_ref[...] = roped[2].T.astype(BF16)
    v = z[:, 3 * LANES:4 * LANES]
    tail = jnp.where(lane == HEAD_DIM, 1.0, 0.0)
    v_ref[0] = jnp.where(low_head, v, tail).astype(BF16)
    v_ref[1] = jnp.where(low_head, pltpu.roll(v, HEAD_DIM, 1), tail).astype(BF16)

    for w_ref, b_ref, o_ref in ((wsc_ref, bsc_ref, zsc_ref), (wcf_ref, bcf_ref, zcf_ref),
                                (wpl_ref, bpl_ref, zpl_ref), (wg_ref, bg_ref, zg_ref)):
        o_ref[...] = (jnp.dot(hb, w_ref[...], preferred_element_type=F32) + b_ref[...]).astype(BF16)


def _projections(x, ln, ws, bs, gain, scale, cos, sin, batch, seq):
    n, d = x.shape
    t = _tile(seq, 256)
    tk = _attn_tiles(seq)[1]
    assert tk % t == 0
    per_seq, per_chunk, nc = seq // t, tk // t, seq // tk
    row = lambda i: (i, 0)
    rope = lambda i: (i % per_seq, 0)
    resident = lambda c: pl.BlockSpec((d, c), lambda i: (0, 0), pipeline_mode=pl.Buffered(1))
    widths = [w.shape[1] for w in ws]
    apply_ln = ln is not None
    ln_args = [ln[0].reshape(1, d), ln[1].reshape(1, d)] if apply_ln else []
    in_specs = ([pl.BlockSpec((t, d), row)] + [_const_spec((1, d))] * len(ln_args)
                + [resident(widths[0]), _const_spec((1, widths[0])), _const_spec((1, 3 * LANES)),
                   _const_spec((1, 3 * LANES)), pl.BlockSpec((t, LANES), rope), pl.BlockSpec((t, LANES), rope)]
                + [resident(c) for c in widths[1:]] + [_const_spec((1, c)) for c in widths[1:]])
    out_specs = ([pl.BlockSpec((t, d), row)] if apply_ln else []) + [
        pl.BlockSpec((t, 2 * LANES), row),
        pl.BlockSpec((None, None, LANES, t), lambda i: (i // per_seq, (i % per_seq) // per_chunk, 0, i % per_chunk)),
        pl.BlockSpec((None, None, N_KV_HEADS, t, LANES),
                     lambda i: (i // per_seq, (i % per_seq) // per_chunk, 0, i % per_chunk, 0)),
    ] + [pl.BlockSpec((t, c), row) for c in widths[1:]]
    out_shape = ([jax.ShapeDtypeStruct((n, d), F32)] if apply_ln else []) + [
        jax.ShapeDtypeStruct((n, 2 * LANES), BF16),
        jax.ShapeDtypeStruct((batch, nc, LANES, tk), BF16),
        jax.ShapeDtypeStruct((batch, nc, N_KV_HEADS, tk, LANES), BF16),
    ] + [jax.ShapeDtypeStruct((n, c), BF16) for c in widths[1:]]
    return pl.pallas_call(
        functools.partial(_proj_kernel, apply_ln),
        grid=(n // t,),
        in_specs=in_specs, out_specs=out_specs, out_shape=out_shape,
        compiler_params=_cparams("parallel"),
        name="projections",
    )(x, *ln_args, ws[0], bs[0], gain, scale, cos, sin, *ws[1:], *bs[1:])


def _attn_kernel(q_ref, kt_ref, v_ref, o_ref, m_scr, acc_scr):
    h = pl.program_id(1)
    tq = q_ref.shape[0]
    rows = Q_GROUP * tq
    n_chunks, _, tk = kt_ref.shape
    lane = lax.broadcasted_iota(I32, (tq, LANES), 1)
    own_half = (lane // HEAD_DIM) == h
    first = h == 0
    qf = q_ref[...].astype(F32)
    swapped = pltpu.roll(qf, HEAD_DIM, 1)
    q = jnp.concatenate([jnp.where(own_half, jnp.where(first, qf, swapped), 0.0),
                         jnp.where(own_half, jnp.where(first, swapped, qf), 0.0)], axis=0).astype(BF16)
    m_scr[...] = jnp.full(m_scr.shape, -jnp.inf, F32)
    acc_scr[...] = jnp.zeros(acc_scr.shape, F32)

    def body(c, carry):
        s = jnp.dot(q, kt_ref[c], preferred_element_type=F32)
        m_prev = m_scr[...]
        m_new = jnp.maximum(m_prev, jnp.max(s, axis=-1, keepdims=True))
        alpha = jnp.exp2(m_prev - m_new)
        p = jnp.exp2(s - jnp.concatenate([m_new] * (tk // LANES), axis=1)).astype(BF16)
        acc_scr[...] = alpha * acc_scr[...] + jnp.dot(p, v_ref[c], preferred_element_type=F32)
        m_scr[...] = m_new
        return carry

    lax.fori_loop(0, n_chunks, body, 0, unroll=2 if n_chunks % 2 == 0 else 1)
    acc = acc_scr[...]
    res = acc / acc[:, HEAD_DIM:HEAD_DIM + 1]
    o_ref[...] = jnp.where(lane < HEAD_DIM, res[:tq], pltpu.roll(res[tq:], HEAD_DIM, 1)).astype(BF16)


def _attention(q, kt, v, batch, seq):
    tq, tk = _attn_tiles(seq)
    nc = seq // tk
    per_seq = seq // tq
    rows = Q_GROUP * tq
    return pl.pallas_call(
        _attn_kernel,
        grid=(batch, N_KV_HEADS, per_seq),
        in_specs=[pl.BlockSpec((tq, LANES), lambda b, h, i: (b * per_seq + i, h)),
                  pl.BlockSpec((None, nc, LANES, tk), lambda b, h, i: (b, 0, 0, 0)),
                  pl.BlockSpec((None, nc, None, tk, LANES), lambda b, h, i: (b, 0, h, 0, 0))],
        out_specs=pl.BlockSpec((None, tq, LANES), lambda b, h, i: (b, i, h)),
        out_shape=jax.ShapeDtypeStruct((batch, seq, N_KV_HEADS * LANES), BF16),
        scratch_shapes=[pltpu.VMEM((rows, LANES), F32), pltpu.VMEM((rows, LANES), F32)],
        compiler_params=_cparams("parallel", "parallel", "arbitrary"),
        name="attention",
    )(q, kt, v).reshape(batch * seq, N_KV_HEADS * LANES)


def _dwconv(xs_ref, w_ref, n_taps, first_row, rows, shifted_ref=None):
    if shifted_ref is None:
        taps = [xs_ref[pl.ds(first_row + j, rows), :] for j in range(n_taps)]
    else:
        span = shifted_ref.shape[1]
        assert first_row + n_taps - 1 + rows <= span + ROW_GROUP - 1
        shifts = sorted({(first_row + j) % ROW_GROUP for j in range(n_taps)} - {0})
        for s in shifts:
            shifted_ref[s] = xs_ref[pl.ds(s, span), :]
        taps = []
        for j in range(n_taps):
            s, base = (first_row + j) % ROW_GROUP, (first_row + j) // ROW_GROUP * ROW_GROUP
            src = xs_ref if s == 0 else shifted_ref.at[s]
            taps.append(src[pl.ds(base, rows), :])
    acc = None
    for j, tap in enumerate(taps):
        term = tap * w_ref[j:j + 1, :]
        acc = term if acc is None else acc + term
    return acc


def _mixer_kernel(seq, alpha,
                  o_ref, scp_ref, scc_ref, scn_ref, cfp_ref, cfc_ref, cfn_ref, plp_ref, plc_ref, pln_ref,
                  zg_ref, h_ref, scw_ref, cfw_ref, cfb_ref, cfg_ref, cfbeta_ref, plmask_ref, pllo_ref, plhi_ref,
                  plw_ref, pls_ref, wbr_ref, wout_ref, ln1g_ref, ln1b_ref, wr_ref, br_ref,
                  h1_ref, rw_ref, rpos_ref, cnt_ref, zt_ref,
                  xsc, xcf, xpl, xsh):
    t = h_ref.shape[0]
    w = BRANCH_W
    i = pl.program_id(0)
    per_seq = seq // t
    j = i % per_seq
    has_prev = (j > 0).astype(F32)
    has_next = (j < per_seq - 1).astype(F32)

    def sc_v(z):
        z = z.astype(F32)
        return z[:, w:2 * w] * z[:, 2 * w:3 * w]
    xsc[0:HALO, :] = sc_v(scp_ref[...]) * has_prev
    xsc[HALO:HALO + t, :] = sc_v(scc_ref[...])
    xsc[HALO + t:2 * HALO + t, :] = sc_v(scn_ref[...]) * has_next
    y_sc = scc_ref[:, 0:w].astype(F32) * _dwconv(xsc, scw_ref, SC_WIDTH, HALO - SC_WIDTH // 2, t)

    def cf_v(z):
        z = z.astype(F32)
        return z[:, 0:w] * _sigmoid(z[:, w:2 * w])
    xcf[0:HALO, :] = cf_v(cfp_ref[...]) * has_prev
    xcf[HALO:HALO + t, :] = cf_v(cfc_ref[...])
    xcf[HALO + t:2 * HALO + t, :] = cf_v(cfn_ref[...]) * has_next
    y_cf = _dwconv(xcf, cfw_ref, CF_WIDTH, HALO - CF_WIDTH // 2, t, xsh) + cfb_ref[...]
    y_cf = _layer_norm_rows(y_cf, cfg_ref[...], cfbeta_ref[...])
    y_cf = y_cf * _sigmoid(y_cf)

    u = plc_ref[...].astype(F32)
    xpl[0:HALO, :] = plp_ref[...].astype(F32) * has_prev
    xpl[HALO:HALO + t, :] = u
    xpl[HALO + t:2 * HALO + t, :] = pln_ref[...].astype(F32) * has_next
    win_sum = _dwconv(xpl, plmask_ref, POOL_TAPS, HALO + POOL_FIRST, t, xsh)
    pos = (j * t + lax.broadcasted_iota(I32, (t, w), 0)).astype(F32)
    n_win = jnp.minimum(pos + plhi_ref[...], float(seq - 1)) - jnp.maximum(pos - pllo_ref[...], 0.0) + 1.0
    y_pl = win_sum / n_win - u
    y_pl = jnp.dot(y_pl.astype(BF16), plw_ref[...], preferred_element_type=F32) * pls_ref[...]

    merged = None
    for g, br in enumerate((o_ref[...], y_sc.astype(BF16), y_cf.astype(BF16), y_pl.astype(BF16))):
        proj = jnp.dot(br, wbr_ref[g], preferred_element_type=F32)
        gate = 1.0 / (1.0 + jnp.exp2(zg_ref[:, g * proj.shape[1]:(g + 1) * proj.shape[1]].astype(F32)))
        merged = gate * proj if merged is None else merged + gate * proj
    mix = jnp.dot(merged.astype(BF16), wout_ref[...], preferred_element_type=F32)
    h1 = _layer_norm_rows(alpha * h_ref[...] + mix, ln1g_ref[...], ln1b_ref[...])
    h1_ref[...] = h1

    logits = jnp.dot(h1.astype(BF16), wr_ref[...], preferred_element_type=F32) + br_ref[...]
    lane = lax.broadcasted_iota(I32, (t, LANES), 1)
    vals = logits
    tops, sels = [], []
    for _ in range(TOP_K):
        m = jnp.max(vals, axis=-1, keepdims=True)
        first = jnp.min(jnp.where(vals == m, lane, LANES), axis=-1, keepdims=True)
        sel = lane == first
        vals = jnp.where(sel, -jnp.inf, vals)
        tops.append(m)
        sels.append(sel)
    exps = [jnp.exp(m - tops[0]) for m in tops]
    denom = exps[0] + exps[1] + exps[2] + exps[3]

    onehot = jnp.where(sels[0] | sels[1] | sels[2] | sels[3], 1.0, 0.0)
    r_id = lax.broadcasted_iota(I32, (t, t), 0)
    c_id = lax.broadcasted_iota(I32, (t, t), 1)
    lower = jnp.where(c_id < r_id, 1.0, 0.0).astype(BF16)
    rank_e = jnp.dot(lower, onehot.astype(BF16), preferred_element_type=F32)
    cnt = jnp.sum(onehot, axis=0, keepdims=True)
    groups = jnp.floor((cnt + (ROW_GROUP - 1.0)) * (1.0 / ROW_GROUP))
    e_r = lax.broadcasted_iota(I32, (LANES, LANES), 0)
    e_c = lax.broadcasted_iota(I32, (LANES, LANES), 1)
    before = jnp.where(e_r < e_c, 1.0, 0.0).astype(BF16)
    strip_start = jnp.dot(jnp.broadcast_to(groups, (ROW_GROUP, LANES)).astype(BF16), before,
                          preferred_element_type=F32)[0:1, :]
    row_e = rank_e + ROW_GROUP * strip_start
    cnt_ref[...] = jnp.broadcast_to(cnt, cnt_ref.shape).astype(I32)

    zr = zt_ref.shape[0]
    z_row = lax.broadcasted_iota(I32, (t, zr), 1)
    rw = jnp.zeros((t, LANES), F32)
    rpos = jnp.zeros((t, LANES), I32)
    scatter = jnp.zeros((t, zr), F32)
    for k in range(TOP_K):
        pos_k = jnp.sum(jnp.where(sels[k], row_e, 0.0), axis=-1, keepdims=True).astype(I32)
        scatter = jnp.where(z_row == pos_k, 1.0, scatter)
        rw = jnp.where(lane == k, exps[k] / denom, rw)
        rpos = jnp.where(lane == k, pos_k, rpos)
    rw_ref[...] = rw
    rpos_ref[...] = rpos
    zt_ref[...] = lax.dot_general(scatter.astype(BF16), h1.astype(BF16), (((0,), (0,)), ((), ())),
                                  preferred_element_type=F32)


def _mixer(o, zsc, zcf, zpl, zg, h, p, seq, alpha):
    n, d = h.shape
    t = _tile(seq, MOE_TILE)
    hb = t // HALO
    n_halo = n // HALO
    row = lambda i: (i, 0)
    prev = lambda i: (jnp.maximum(i * hb - 1, 0), 0)
    nxt = lambda i: (jnp.minimum((i + 1) * hb, n_halo - 1), 0)

    def halo_specs(c):
        return [pl.BlockSpec((HALO, c), prev), pl.BlockSpec((t, c), row), pl.BlockSpec((HALO, c), nxt)]

    consts = [p["sc_w"], p["cf_w"], p["cf_b"], p["cf_g"], p["cf_beta"], p["pl_mask"], p["pl_lo"], p["pl_hi"],
              p["pl_w"], p["pl_s"], p["w_br"], p["w_out"], p["ln1_g"], p["ln1_b"], p["w_r"], p["b_r"]]
    lane_out = lambda dt: jax.ShapeDtypeStruct((n, LANES), dt)
    n_tiles = n // t
    zr = _tile_sorted_rows(t)
    return pl.pallas_call(
        functools.partial(_mixer_kernel, seq, alpha),
        grid=(n_tiles,),
        in_specs=[pl.BlockSpec((t, BRANCH_W), row)] + halo_specs(3 * BRANCH_W) + halo_specs(2 * BRANCH_W)
        + halo_specs(BRANCH_W) + [pl.BlockSpec((t, N_BRANCHES * d), row), pl.BlockSpec((t, d), row)]
        + [_const_spec(c.shape) for c in consts],
        out_specs=[pl.BlockSpec((t, d), row), pl.BlockSpec((t, LANES), row),
                   pl.BlockSpec((t, LANES), row), pl.BlockSpec((None, ROW_GROUP, LANES), lambda i: (i, 0, 0)),
                   pl.BlockSpec((zr, d), row)],
        out_shape=[jax.ShapeDtypeStruct((n, d), F32), lane_out(F32), lane_out(I32),
                   jax.ShapeDtypeStruct((n_tiles, ROW_GROUP, LANES), I32),
                   jax.ShapeDtypeStruct((n_tiles * zr, d), F32)],
        scratch_shapes=[pltpu.VMEM((t + 2 * HALO, BRANCH_W), F32)] * 3
        + [pltpu.VMEM((ROW_GROUP, t + 2 * HALO - ROW_GROUP, BRANCH_W), F32)],
        compiler_params=_cparams("parallel"),
        name="mixer",
    )(o, zsc, zsc, zsc, zcf, zcf, zcf, zpl, zpl, zpl, zg, h, *consts)


def _group_copy(src_hbm, group, dst, dst_group, sem):
    return pltpu.make_async_copy(src_hbm.at[pl.ds(pl.multiple_of(group * ROW_GROUP, ROW_GROUP), ROW_GROUP), :],
                                 dst.at[pl.ds(dst_group * ROW_GROUP, ROW_GROUP), :], sem)


def _cast_rows(src_ref, dst_ref, chunk):
    def body(r, c):
        rows = pl.ds(pl.multiple_of(r * chunk, chunk), chunk)
        dst_ref[rows, :] = src_ref[rows, :].astype(dst_ref.dtype)
        return c
    lax.fori_loop(0, src_ref.shape[0] // chunk, body, 0)


def _expert_kernel(blk_e_ref, n_used_ref, src_ref, zt_hbm, wgu32_ref, bgu_ref, wd32_ref, bd_ref, y_ref,
                   xbuf, wgu_ref, wd_ref, sem):
    b = pl.program_id(0)
    n_used = n_used_ref[0]
    groups = MOE_BLOCK // ROW_GROUP

    @pl.when((b < n_used) & ((b == 0) | (blk_e_ref[b] != blk_e_ref[jnp.maximum(b - 1, 0)])))
    def _():
        _cast_rows(wgu32_ref, wgu_ref, LANES)
        _cast_rows(wd32_ref, wd_ref, LANES)

    def gather(block, slot):
        block = jnp.minimum(block, n_used - 1)
        for g in range(groups):
            _group_copy(zt_hbm, src_ref[block * groups + g], xbuf.at[slot], g, sem.at[slot]).start()

    def wait(slot):
        pltpu.make_async_copy(xbuf.at[slot], xbuf.at[slot], sem.at[slot]).wait()

    @pl.when(b == 0)
    def _():
        gather(0, 0)
        gather(1, 1)

    @pl.when(b < n_used)
    def _():
        wait(b % PREFETCH_DEPTH)
        f = wd_ref.shape[0]
        x = xbuf[b % PREFETCH_DEPTH].astype(BF16)
        hgu = jnp.dot(x, wgu_ref[...], preferred_element_type=F32) + bgu_ref[...]
        glu = jnp.minimum(hgu[:, :f], SWIGLU_LIMIT)
        lin = jnp.clip(hgu[:, f:], -SWIGLU_LIMIT, SWIGLU_LIMIT)
        act = glu * _sigmoid(SWIGLU_ALPHA * glu) * (lin + 1.0)
        y_ref[...] = jnp.dot(act.astype(BF16), wd_ref[...], preferred_element_type=F32) + bd_ref[...]
        gather(b + 2, (b + 2) % PREFETCH_DEPTH)

    @pl.when(b == n_used - 1)
    def _():
        wait((b + 1) % PREFETCH_DEPTH)
        wait((b + 2) % PREFETCH_DEPTH)

    @pl.when(b >= n_used)
    def _():
        y_ref[...] = jnp.zeros(y_ref.shape, F32)


def _experts(zt, blk_e, n_used, src_groups, n_blocks, layer, wgu, bgu, wd, bd):
    d = zt.shape[1]
    f2 = wgu.shape[3]
    expert = lambda b, be, nu, sg: (layer, be[b], 0, 0)
    return pl.pallas_call(
        _expert_kernel,
        grid_spec=pltpu.PrefetchScalarGridSpec(
            num_scalar_prefetch=3,
            grid=(n_blocks,),
            in_specs=[pl.BlockSpec(memory_space=pl.ANY),
                      pl.BlockSpec((None, None, d, f2), expert), pl.BlockSpec((None, None, 1, f2), expert),
                      pl.BlockSpec((None, None, f2 // 2, d), expert), pl.BlockSpec((None, None, 1, d), expert)],
            out_specs=pl.BlockSpec((MOE_BLOCK, d), lambda b, be, nu, sg: (b, 0)),
            scratch_shapes=[pltpu.VMEM((PREFETCH_DEPTH, MOE_BLOCK, d), F32), pltpu.VMEM((d, f2), BF16),
                            pltpu.VMEM((f2 // 2, d), BF16), pltpu.SemaphoreType.DMA((PREFETCH_DEPTH,))],
        ),
        out_shape=jax.ShapeDtypeStruct((n_blocks * MOE_BLOCK, d), F32),
        compiler_params=_cparams("arbitrary"),
        name="moe_experts",
    )(blk_e, n_used, src_groups, zt, wgu, bgu, wd, bd)


def _combine_kernel(alpha, src_ref, y_hbm, rw_ref, rpos_ref, h1_ref, g_ref, b_ref, h_ref, buf, sem):
    i = pl.program_id(0)
    n_steps = pl.num_programs(0)
    zr = buf.shape[1]
    groups = zr // ROW_GROUP
    t = h1_ref.shape[0]

    def gather(tile, slot):
        tile = jnp.minimum(tile, n_steps - 1)
        for g in range(groups):
            _group_copy(y_hbm, src_ref[tile * groups + g], buf.at[slot], g, sem.at[slot]).start(priority=g % 2)

    def wait(slot):
        pltpu.make_async_copy(buf.at[slot], buf.at[slot], sem.at[slot]).wait()

    @pl.when(i == 0)
    def _():
        gather(0, 0)
        gather(1, 1)

    slot = i % PREFETCH_DEPTH
    wait(slot)
    lane = lax.broadcasted_iota(I32, (t, LANES), 1)
    z_row = lax.broadcasted_iota(I32, (t, zr), 1)
    rw = rw_ref[...]
    rpos = rpos_ref[...]
    weights = jnp.zeros((t, zr), F32)
    for k in range(TOP_K):
        w_k = jnp.sum(jnp.where(lane == k, rw, 0.0), axis=-1, keepdims=True)
        pos_k = jnp.sum(jnp.where(lane == k, rpos, 0), axis=-1, keepdims=True)
        weights = jnp.where(z_row == pos_k, w_k, weights)
    ffn = jnp.dot(weights.astype(BF16), buf[slot].astype(BF16), preferred_element_type=F32)
    h_ref[...] = _layer_norm_rows(alpha * h1_ref[...] + ffn, g_ref[...], b_ref[...])
    gather(i + 2, (i + 2) % PREFETCH_DEPTH)

    @pl.when(i == n_steps - 1)
    def _():
        wait((i + 1) % PREFETCH_DEPTH)
        wait((i + 2) % PREFETCH_DEPTH)


def _combine(y, src_groups, rw, rpos, h1, g, b, alpha, tile):
    n, d = h1.shape
    zr = _tile_sorted_rows(tile)
    row = lambda i, sg: (i, 0)
    const = lambda i, sg: (0, 0)
    return pl.pallas_call(
        functools.partial(_combine_kernel, alpha),
        grid_spec=pltpu.PrefetchScalarGridSpec(
            num_scalar_prefetch=1,
            grid=(n // tile,),
            in_specs=[pl.BlockSpec(memory_space=pl.ANY), pl.BlockSpec((tile, LANES), row),
                      pl.BlockSpec((tile, LANES), row), pl.BlockSpec((tile, d), row),
                      pl.BlockSpec((1, d), const), pl.BlockSpec((1, d), const)],
            out_specs=pl.BlockSpec((tile, d), row),
            scratch_shapes=[pltpu.VMEM((PREFETCH_DEPTH, zr, d), F32), pltpu.SemaphoreType.DMA((PREFETCH_DEPTH,))],
        ),
        out_shape=jax.ShapeDtypeStruct((n, d), F32),
        compiler_params=_cparams("arbitrary"),
        name="moe_combine",
    )(src_groups, y, rw, rpos, h1, g.reshape(1, d), b.reshape(1, d))


def _rope_tables(seq):
    rows = seq // GRID_W
    row = jnp.repeat(jnp.arange(rows, dtype=F32), GRID_W)
    col = jnp.tile(jnp.arange(GRID_W, dtype=F32), rows)
    inv = ROPE_THETA ** (-jnp.arange(0, ROPE_HALF, 2, dtype=F32) / ROPE_HALF)
    ang_r = row[:, None] * inv
    ang_c = col[:, None] * inv
    cr, sr, cc, sc = jnp.cos(ang_r), jnp.sin(ang_r), jnp.cos(ang_c), jnp.sin(ang_c)
    cos = jnp.concatenate([cr, cr, cc, cc], axis=-1)
    sin = jnp.concatenate([-sr, sr, -sc, sc], axis=-1)
    return jnp.tile(cos, (1, 2)), jnp.tile(sin, (1, 2))


def _pool_constants():
    offs = np.arange(POOL_TAPS) + POOL_FIRST
    mask = np.zeros((POOL_TAPS, BRANCH_W), np.float32)
    lo = np.zeros((1, BRANCH_W), np.float32)
    hi = np.zeros((1, BRANCH_W), np.float32)
    for g, win in enumerate(POOL_WINDOWS):
        sl = slice(g * POOL_GROUP, (g + 1) * POOL_GROUP)
        mask[(offs >= -(win // 2)) & (offs <= win - 1 - win // 2), sl] = 1.0
        lo[0, sl] = win // 2
        hi[0, sl] = win - 1 - win // 2
    return jnp.asarray(mask), jnp.asarray(lo), jnp.asarray(hi)


def _block_diag(blocks):
    g, a, b = blocks.shape
    out = jnp.zeros((g * a, g * b), blocks.dtype)
    for i in range(g):
        out = out.at[i * a:(i + 1) * a, i * b:(i + 1) * b].set(blocks[i])
    return out


def _moe_tables(tile_counts, tile, n_blocks):
    nt = tile_counts.shape[0]
    bg = MOE_BLOCK // ROW_GROUP
    gz = _tile_sorted_rows(tile) // ROW_GROUP
    ng = (tile_counts + ROW_GROUP - 1) // ROW_GROUP
    strip_end = jnp.cumsum(ng, axis=1)
    strip_start = strip_end - ng
    run_end = jnp.cumsum(ng, axis=0)
    run_start = run_end - ng
    total = run_end[-1]
    blocks_e = (total + bg - 1) // bg
    blk_end = jnp.cumsum(blocks_e)
    blk_start = blk_end - blocks_e
    n_used = blk_end[-1:].astype(I32)
    b_ids = jnp.arange(n_blocks, dtype=I32)
    blk_e = jnp.minimum(jnp.sum(blk_end[None, :] <= b_ids[:, None], axis=1), N_EXPERTS - 1).astype(I32)

    experts = jnp.arange(N_EXPERTS, dtype=I32)
    is_e = blk_e[:, None] == experts[None, :]
    pick = lambda tab: jnp.sum(jnp.where(is_e, tab[None, :], 0), axis=1)
    columns = lambda tab: jnp.dot(is_e.astype(F32), tab.T.astype(F32),
                                  precision=lax.Precision.HIGHEST).astype(I32)
    q = (b_ids - pick(blk_start))[:, None] * bg + jnp.arange(bg, dtype=I32)[None, :]
    starts, ends = columns(run_start), columns(run_end)
    offset = columns(jnp.arange(nt, dtype=I32)[:, None] * gz + strip_start - run_start)
    in_strip = (starts[:, None, :] <= q[:, :, None]) & (q[:, :, None] < ends[:, None, :])
    z_src = jnp.where(q < pick(total)[:, None],
                      jnp.sum(jnp.where(in_strip, offset[:, None, :], 0), axis=2) + q, 0)

    j = jnp.arange(gz, dtype=I32)[None, :]
    e_j = jnp.minimum(jnp.sum(strip_end[:, None, :] <= j[:, :, None], axis=2), N_EXPERTS - 1)
    base = blk_start[None, :] * bg + run_start - strip_start
    picked = jnp.sum(jnp.where(e_j[:, :, None] == experts[None, None, :], base[:, None, :], 0), axis=2)
    y_src = jnp.where(j < strip_end[:, -1:], picked + j, 0)
    return blk_e, n_used, z_src.reshape(-1).astype(I32), y_src.reshape(-1).astype(I32)


def kernel(x, ln_in_g, ln_in_b, w_in, b_in, q_norm_g, k_norm_g, sc_conv_w, cf_conv_w, cf_conv_b, cf_ln_g, cf_ln_b, pool_w, pool_scale, w_branch, w_out, ln1_g, ln1_b, w_router, b_router, w_gate_up, b_gate_up, w_down, b_down, ln2_g, ln2_b):
    batch, seq, d = x.shape
    depth = w_in.shape[0]
    n = batch * seq
    alpha = float((2.0 * depth) ** 0.25)
    tile = _tile(seq, MOE_TILE)
    n_blocks = pl.cdiv((n // tile) * _tile_sorted_rows(tile), MOE_BLOCK) + N_EXPERTS

    cos, sin = _rope_tables(seq)
    pl_mask, pl_lo, pl_hi = _pool_constants()
    q_scale = jnp.concatenate([jnp.full((1, 2 * LANES), HEAD_DIM ** -0.5 * np.log2(np.e), F32),
                               jnp.ones((1, LANES), F32)], axis=-1)
    splits = np.cumsum([QKV_COLS, 3 * BRANCH_W, 2 * BRANCH_W, BRANCH_W])

    h = x.reshape(n, d)
    for l in range(depth):
        col_scale = jnp.concatenate([jnp.ones((splits[3],), F32), jnp.full((w_in.shape[2] - splits[3],), -np.log2(np.e), F32)])
        wl = (w_in[l] * col_scale).astype(BF16)
        bl = (b_in[l] * col_scale).reshape(1, -1)
        w_parts = [wl[:, :splits[0]]] + [wl[:, splits[i]:splits[i + 1]] for i in range(3)] + [wl[:, splits[3]:]]
        b_parts = [bl[:, :splits[0]]] + [bl[:, splits[i]:splits[i + 1]] for i in range(3)] + [bl[:, splits[3]:]]
        gain = jnp.concatenate([jnp.tile(q_norm_g[l], N_Q_HEADS), jnp.tile(k_norm_g[l], N_KV_HEADS)]).reshape(1, -1)

        outs = _projections(h, (ln_in_g, ln_in_b) if l == 0 else None, w_parts, b_parts, gain, q_scale, cos, sin,
                            batch, seq)
        if l == 0:
            h, *outs = outs
        q, kt, v, zsc, zcf, zpl, zg = outs
        o = _attention(q, kt, v, batch, seq)

        r_pad = LANES - N_EXPERTS
        params = dict(
            sc_w=sc_conv_w[l], cf_w=cf_conv_w[l], cf_b=cf_conv_b[l].reshape(1, -1), cf_g=cf_ln_g[l].reshape(1, -1),
            cf_beta=cf_ln_b[l].reshape(1, -1), pl_mask=pl_mask, pl_lo=pl_lo, pl_hi=pl_hi,
            pl_w=_block_diag(pool_w[l]).astype(BF16), pl_s=pool_scale[l].reshape(1, -1),
            w_br=w_branch[l].astype(BF16), w_out=w_out[l].astype(BF16),
            ln1_g=ln1_g[l].reshape(1, -1), ln1_b=ln1_b[l].reshape(1, -1),
            w_r=jnp.pad(w_router[l], ((0, 0), (0, r_pad))).astype(BF16),
            b_r=jnp.pad(b_router[l].astype(F32), (0, r_pad), constant_values=NEG_BIG).reshape(1, -1),
        )
        h1, rw, rpos, cnt, zt = _mixer(o, zsc, zcf, zpl, zg, h, params, seq, alpha)

        blk_e, n_used, z_src, y_src = _moe_tables(cnt[:, 0, :N_EXPERTS], tile, n_blocks)
        y = _experts(zt, blk_e, n_used, z_src, n_blocks, l, w_gate_up, b_gate_up[:, :, None, :],
                     w_down, b_down[:, :, None, :])
        h = _combine(y, y_src, rw, rpos, h1, ln2_g[l], ln2_b[l], alpha, tile)
    return h.reshape(batch, seq, d)
```

```python
import functools

import jax
import jax.numpy as jnp
import numpy as np
from jax import lax
from jax.experimental import pallas as pl
from jax.experimental.pallas import tpu as pltpu

F32 = jnp.float32
BF16 = jnp.bfloat16
I32 = jnp.int32
U32 = jnp.uint32

GRID_W = 64
HEAD_DIM = 64
N_Q_HEADS = 4
N_KV_HEADS = 2
Q_GROUP = N_Q_HEADS // N_KV_HEADS
ROPE_HALF = HEAD_DIM // 2
ROPE_THETA = 10000.0
BRANCH_W = 256
N_BRANCHES = 4
SC_WIDTH = 3
CF_WIDTH = 31
POOL_WINDOWS = (2, 4, 8, 16)
POOL_GROUP = BRANCH_W // len(POOL_WINDOWS)
POOL_TAPS = 16
POOL_FIRST = -8
N_EXPERTS = 32
TOP_K = 4
SWIGLU_LIMIT = 7.0
SWIGLU_ALPHA = 1.702
LN_EPS = 1e-5
RMS_EPS = 1e-6
QKV_COLS = N_Q_HEADS * HEAD_DIM + 2 * N_KV_HEADS * HEAD_DIM

LANES = 128
HALO = 16
ROW_GROUP = 8
MOE_TILE = 256
MOE_BLOCK = 512
PREFETCH_DEPTH = 3
VMEM_LIMIT = 56 * 1024 * 1024
NEG_BIG = -1e30


def _tile_sorted_rows(t):
    return t * TOP_K + N_EXPERTS * ROW_GROUP


def _cparams(*sem):
    return pltpu.CompilerParams(dimension_semantics=sem, vmem_limit_bytes=VMEM_LIMIT)


def _tile(n, pref):
    t = min(n, pref)
    assert n % t == 0, (n, t)
    return t


def _const_spec(shape):
    nd = len(shape)
    return pl.BlockSpec(shape, lambda *_: (0,) * nd)


def _layer_norm_rows(y, g, b):
    mu = jnp.mean(y, axis=-1, keepdims=True)
    d = y - mu
    var = jnp.mean(d * d, axis=-1, keepdims=True)
    return d * lax.rsqrt(var + LN_EPS) * g + b


def _sigmoid(x):
    return 1.0 / (1.0 + jnp.exp(-x))


def _attn_tiles(seq):
    return _tile(seq, 1024), _tile(seq, 2048)


def _proj_kernel(apply_ln, *refs):
    refs = list(refs)
    x_ref = refs.pop(0)
    ln_refs = [refs.pop(0), refs.pop(0)] if apply_ln else None
    (wqkv_ref, bqkv_ref, gain_ref, scale_ref, cos_ref, sin_ref,
     wsc_ref, wcf_ref, wpl_ref, wg_ref, bsc_ref, bcf_ref, bpl_ref, bg_ref) = refs[:14]
    outs = refs[14:]
    h = x_ref[...]
    if apply_ln:
        h = _layer_norm_rows(h, ln_refs[0][...], ln_refs[1][...])
        outs.pop(0)[...] = h
    q_ref, kt_ref, v_ref, zsc_ref, zcf_ref, zpl_ref, zg_ref = outs
    hb = h.astype(BF16)

    z = jnp.dot(hb, wqkv_ref[...], preferred_element_type=F32) + bqkv_ref[...]
    t = z.shape[0]
    lane = lax.broadcasted_iota(I32, (t, LANES), 1)
    low_head = lane < HEAD_DIM
    first_half = (lane % ROPE_HALF) < (ROPE_HALF // 2)
    cos = cos_ref[...]
    sin = sin_ref[...]
    roped = []
    for c in range(3):
        x = z[:, c * LANES:(c + 1) * LANES]
        x2 = x * x
        s_all = jnp.sum(x2, axis=-1, keepdims=True)
        s_low = jnp.sum(jnp.where(low_head, x2, 0.0), axis=-1, keepdims=True)
        ms = jnp.where(low_head, s_low, s_all - s_low) * (1.0 / HEAD_DIM)
        xn = x * lax.rsqrt(ms + RMS_EPS) * gain_ref[:, c * LANES:(c + 1) * LANES]
        partner = jnp.where(first_half, pltpu.roll(xn, LANES - ROPE_HALF // 2, 1), pltpu.roll(xn, ROPE_HALF // 2, 1))
        roped.append((xn * cos + partner * sin) * scale_ref[:, c * LANES:(c + 1) * LANES])
    q_ref[:, 0:LANES] = roped[0].astype(BF16)
    q_ref[:, LANES:2 * LANES] = roped[1].astype(BF16)
    kt_ref[...] = roped[2].T.astype(BF16)
    v = z[:, 3 * LANES:4 * LANES]
    tail = jnp.where(lane == HEAD_DIM, 1.0, 0.0)
    v_ref[0] = jnp.where(low_head, v, tail).astype(BF16)
    v_ref[1] = jnp.where(low_head, pltpu.roll(v, HEAD_DIM, 1), tail).astype(BF16)

    for w_ref, b_ref, o_ref in ((wsc_ref, bsc_ref, zsc_ref), (wcf_ref, bcf_ref, zcf_ref),
                                (wpl_ref, bpl_ref, zpl_ref), (wg_ref, bg_ref, zg_ref)):
        o_ref[...] = (jnp.dot(hb, w_ref[...], preferred_element_type=F32) + b_ref[...]).astype(BF16)


def _projections(x, ln, ws, bs, gain, scale, cos, sin, batch, seq):
    n, d = x.shape
    t = _tile(seq, 256)
    tk = _attn_tiles(seq)[1]
    assert tk % t == 0
    per_seq, per_chunk, nc = seq // t, tk // t, seq // tk
    row = lambda i: (i, 0)
    rope = lambda i: (i % per_seq, 0)
    resident = lambda c: pl.BlockSpec((d, c), lambda i: (0, 0), pipeline_mode=pl.Buffered(1))
    widths = [w.shape[1] for w in ws]
    apply_ln = ln is not None
    ln_args = [ln[0].reshape(1, d), ln[1].reshape(1, d)] if apply_ln else []
    in_specs = ([pl.BlockSpec((t, d), row)] + [_const_spec((1, d))] * len(ln_args)
                + [resident(widths[0]), _const_spec((1, widths[0])), _const_spec((1, 3 * LANES)),
                   _const_spec((1, 3 * LANES)), pl.BlockSpec((t, LANES), rope), pl.BlockSpec((t, LANES), rope)]
                + [resident(c) for c in widths[1:]] + [_const_spec((1, c)) for c in widths[1:]])
    out_specs = ([pl.BlockSpec((t, d), row)] if apply_ln else []) + [
        pl.BlockSpec((t, 2 * LANES), row),
        pl.BlockSpec((None, None, LANES, t), lambda i: (i // per_seq, (i % per_seq) // per_chunk, 0, i % per_chunk)),
        pl.BlockSpec((None, None, N_KV_HEADS, t, LANES),
                     lambda i: (i // per_seq, (i % per_seq) // per_chunk, 0, i % per_chunk, 0)),
    ] + [pl.BlockSpec((t, c), row) for c in widths[1:]]
    out_shape = ([jax.ShapeDtypeStruct((n, d), F32)] if apply_ln else []) + [
        jax.ShapeDtypeStruct((n, 2 * LANES), BF16),
        jax.ShapeDtypeStruct((batch, nc, LANES, tk), BF16),
        jax.ShapeDtypeStruct((batch, nc, N_KV_HEADS, tk, LANES), BF16),
    ] + [jax.ShapeDtypeStruct((n, c), BF16) for c in widths[1:]]
    return pl.pallas_call(
        functools.partial(_proj_kernel, apply_ln),
        grid=(n // t,),
        in_specs=in_specs, out_specs=out_specs, out_shape=out_shape,
        compiler_params=_cparams("parallel"),
        name="projections",
    )(x, *ln_args, ws[0], bs[0], gain, scale, cos, sin, *ws[1:], *bs[1:])


def _attn_kernel(q_ref, kt_ref, v_ref, o_ref, m_scr, acc_scr):
    h = pl.program_id(1)
    tq = q_ref.shape[0]
    rows = Q_GROUP * tq
    n_chunks, _, tk = kt_ref.shape
    lane = lax.broadcasted_iota(I32, (tq, LANES), 1)
    own_half = (lane // HEAD_DIM) == h
    first = h == 0
    qf = q_ref[...].astype(F32)
    swapped = pltpu.roll(qf, HEAD_DIM, 1)
    q = jnp.concatenate([jnp.where(own_half, jnp.where(first, qf, swapped), 0.0),
                         jnp.where(own_half, jnp.where(first, swapped, qf), 0.0)], axis=0).astype(BF16)
    m_scr[...] = jnp.full(m_scr.shape, -jnp.inf, F32)
    acc_scr[...] = jnp.zeros(acc_scr.shape, F32)

    def body(c, carry):
        s = jnp.dot(q, kt_ref[c], preferred_element_type=F32)
        m_prev = m_scr[...]
        m_new = jnp.maximum(m_prev, jnp.max(s, axis=-1, keepdims=True))
        alpha = jnp.exp2(m_prev - m_new)
        p = jnp.exp2(s - jnp.concatenate([m_new] * (tk // LANES), axis=1)).astype(BF16)
        acc_scr[...] = alpha * acc_scr[...] + jnp.dot(p, v_ref[c], preferred_element_type=F32)
        m_scr[...] = m_new
        return carry

    lax.fori_loop(0, n_chunks, body, 0, unroll=2 if n_chunks % 2 == 0 else 1)
    acc = acc_scr[...]
    res = acc / acc[:, HEAD_DIM:HEAD_DIM + 1]
    o_ref[...] = jnp.where(lane < HEAD_DIM, res[:tq], pltpu.roll(res[tq:], HEAD_DIM, 1)).astype(BF16)


def _attention(q, kt, v, batch, seq):
    tq, tk = _attn_tiles(seq)
    nc = seq // tk
    per_seq = seq // tq
    rows = Q_GROUP * tq
    return pl.pallas_call(
        _attn_kernel,
        grid=(batch, N_KV_HEADS, per_seq),
        in_specs=[pl.BlockSpec((tq, LANES), lambda b, h, i: (b * per_seq + i, h)),
                  pl.BlockSpec((None, nc, LANES, tk), lambda b, h, i: (b, 0, 0, 0)),
                  pl.BlockSpec((None, nc, None, tk, LANES), lambda b, h, i: (b, 0, h, 0, 0))],
        out_specs=pl.BlockSpec((None, tq, LANES), lambda b, h, i: (b, i, h)),
        out_shape=jax.ShapeDtypeStruct((batch, seq, N_KV_HEADS * LANES), BF16),
        scratch_shapes=[pltpu.VMEM((rows, LANES), F32), pltpu.VMEM((rows, LANES), F32)],
        compiler_params=_cparams("parallel", "parallel", "arbitrary"),
        name="attention",
    )(q, kt, v).reshape(batch * seq, N_KV_HEADS * LANES)


def _dwconv(xs_ref, w_ref, n_taps, first_row, rows, shifted_ref=None):
    if shifted_ref is None:
        taps = [xs_ref[pl.ds(first_row + j, rows), :] for j in range(n_taps)]
    else:
        span = shifted_ref.shape[1]
        assert first_row + n_taps - 1 + rows <= span + ROW_GROUP - 1
        shifts = sorted({(first_row + j) % ROW_GROUP for j in range(n_taps)} - {0})
        for s in shifts:
            shifted_ref[s] = xs_ref[pl.ds(s, span), :]
        taps = []
        for j in range(n_taps):
            s, base = (first_row + j) % ROW_GROUP, (first_row + j) // ROW_GROUP * ROW_GROUP
            src = xs_ref if s == 0 else shifted_ref.at[s]
            taps.append(src[pl.ds(base, rows), :])
    acc = None
    for j, tap in enumerate(taps):
        term = tap * w_ref[j:j + 1, :]
        acc = term if acc is None else acc + term
    return acc


def _mixer_kernel(seq, alpha,
                  o_ref, scp_ref, scc_ref, scn_ref, cfp_ref, cfc_ref, cfn_ref, plp_ref, plc_ref, pln_ref,
                  zg_ref, h_ref, scw_ref, cfw_ref, cfb_ref, cfg_ref, cfbeta_ref, plmask_ref, pllo_ref, plhi_ref,
                  plw_ref, pls_ref, wbr_ref, wout_ref, ln1g_ref, ln1b_ref, wr_ref, br_ref,
                  h1_ref, rw_ref, rpos_ref, cnt_ref, zt_ref,
                  xsc, xcf, xpl, xsh):
    t = h_ref.shape[0]
    w = BRANCH_W
    i = pl.program_id(0)
    per_seq = seq // t
    j = i % per_seq
    has_prev = (j > 0).astype(F32)
    has_next = (j < per_seq - 1).astype(F32)

    def sc_v(z):
        z = z.astype(F32)
        return z[:, w:2 * w] * z[:, 2 * w:3 * w]
    xsc[0:HALO, :] = sc_v(scp_ref[...]) * has_prev
    xsc[HALO:HALO + t, :] = sc_v(scc_ref[...])
    xsc[HALO + t:2 * HALO + t, :] = sc_v(scn_ref[...]) * has_next
    y_sc = scc_ref[:, 0:w].astype(F32) * _dwconv(xsc, scw_ref, SC_WIDTH, HALO - SC_WIDTH // 2, t)

    def cf_v(z):
        z = z.astype(F32)
        return z[:, 0:w] * _sigmoid(z[:, w:2 * w])
    xcf[0:HALO, :] = cf_v(cfp_ref[...]) * has_prev
    xcf[HALO:HALO + t, :] = cf_v(cfc_ref[...])
    xcf[HALO + t:2 * HALO + t, :] = cf_v(cfn_ref[...]) * has_next
    y_cf = _dwconv(xcf, cfw_ref, CF_WIDTH, HALO - CF_WIDTH // 2, t, xsh) + cfb_ref[...]
    y_cf = _layer_norm_rows(y_cf, cfg_ref[...], cfbeta_ref[...])
    y_cf = y_cf * _sigmoid(y_cf)

    u = plc_ref[...].astype(F32)
    xpl[0:HALO, :] = plp_ref[...].astype(F32) * has_prev
    xpl[HALO:HALO + t, :] = u
    xpl[HALO + t:2 * HALO + t, :] = pln_ref[...].astype(F32) * has_next
    win_sum = _dwconv(xpl, plmask_ref, POOL_TAPS, HALO + POOL_FIRST, t, xsh)
    pos = (j * t + lax.broadcasted_iota(I32, (t, w), 0)).astype(F32)
    n_win = jnp.minimum(pos + plhi_ref[...], float(seq - 1)) - jnp.maximum(pos - pllo_ref[...], 0.0) + 1.0
    y_pl = win_sum / n_win - u
    y_pl = jnp.dot(y_pl.astype(BF16), plw_ref[...], preferred_element_type=F32) * pls_ref[...]

    merged = None
    for g, br in enumerate((o_ref[...], y_sc.astype(BF16), y_cf.astype(BF16), y_pl.astype(BF16))):
        proj = jnp.dot(br, wbr_ref[g], preferred_element_type=F32)
        gate = 1.0 / (1.0 + jnp.exp2(zg_ref[:, g * proj.shape[1]:(g + 1) * proj.shape[1]].astype(F32)))
        merged = gate * proj if merged is None else merged + gate * proj
    mix = jnp.dot(merged.astype(BF16), wout_ref[...], preferred_element_type=F32)
    h1 = _layer_norm_rows(alpha * h_ref[...] + mix, ln1g_ref[...], ln1b_ref[...])
    h1_ref[...] = h1

    logits = jnp.dot(h1.astype(BF16), wr_ref[...], preferred_element_type=F32) + br_ref[...]
    lane = lax.broadcasted_iota(I32, (t, LANES), 1)
    vals = logits
    tops, sels = [], []
    for _ in range(TOP_K):
        m = jnp.max(vals, axis=-1, keepdims=True)
        first = jnp.min(jnp.where(vals == m, lane, LANES), axis=-1, keepdims=True)
        sel = lane == first
        vals = jnp.where(sel, -jnp.inf, vals)
        tops.append(m)
        sels.append(sel)
    exps = [jnp.exp(m - tops[0]) for m in tops]
    denom = exps[0] + exps[1] + exps[2] + exps[3]

    onehot = jnp.where(sels[0] | sels[1] | sels[2] | sels[3], 1.0, 0.0)
    r_id = lax.broadcasted_iota(I32, (t, t), 0)
    c_id = lax.broadcasted_iota(I32, (t, t), 1)
    lower = jnp.where(c_id < r_id, 1.0, 0.0).astype(BF16)
    rank_e = jnp.dot(lower, onehot.astype(BF16), preferred_element_type=F32)
    cnt = jnp.sum(onehot, axis=0, keepdims=True)
    groups = jnp.floor((cnt + (ROW_GROUP - 1.0)) * (1.0 / ROW_GROUP))
    e_r = lax.broadcasted_iota(I32, (LANES, LANES), 0)
    e_c = lax.broadcasted_iota(I32, (LANES, LANES), 1)
    before = jnp.where(e_r < e_c, 1.0, 0.0).astype(BF16)
    strip_start = jnp.dot(jnp.broadcast_to(groups, (ROW_GROUP, LANES)).astype(BF16), before,
                          preferred_element_type=F32)[0:1, :]
    row_e = rank_e + ROW_GROUP * strip_start
    cnt_ref[...] = jnp.broadcast_to(cnt, cnt_ref.shape).astype(I32)

    zr = zt_ref.shape[0]
    z_row = lax.broadcasted_iota(I32, (t, zr), 1)
    rw = jnp.zeros((t, LANES), F32)
    rpos = jnp.zeros((t, LANES), I32)
    scatter = jnp.zeros((t, zr), F32)
    for k in range(TOP_K):
        pos_k = jnp.sum(jnp.where(sels[k], row_e, 0.0), axis=-1, keepdims=True).astype(I32)
        scatter = jnp.where(z_row == pos_k, 1.0, scatter)
        rw = jnp.where(lane == k, exps[k] / denom, rw)
        rpos = jnp.where(lane == k, pos_k, rpos)
    rw_ref[...] = rw
    rpos_ref[...] = rpos
    zt_ref[...] = lax.dot_general(scatter.astype(BF16), h1.astype(BF16), (((0,), (0,)), ((), ())),
                                  preferred_element_type=F32)


def _mixer(o, zsc, zcf, zpl, zg, h, p, seq, alpha):
    n, d = h.shape
    t = _tile(seq, MOE_TILE)
    hb = t // HALO
    n_halo = n // HALO
    row = lambda i: (i, 0)
    prev = lambda i: (jnp.maximum(i * hb - 1, 0), 0)
    nxt = lambda i: (jnp.minimum((i + 1) * hb, n_halo - 1), 0)

    def halo_specs(c):
        return [pl.BlockSpec((HALO, c), prev), pl.BlockSpec((t, c), row), pl.BlockSpec((HALO, c), nxt)]

    consts = [p["sc_w"], p["cf_w"], p["cf_b"], p["cf_g"], p["cf_beta"], p["pl_mask"], p["pl_lo"], p["pl_hi"],
              p["pl_w"], p["pl_s"], p["w_br"], p["w_out"], p["ln1_g"], p["ln1_b"], p["w_r"], p["b_r"]]
    lane_out = lambda dt: jax.ShapeDtypeStruct((n, LANES), dt)
    n_tiles = n // t
    zr = _tile_sorted_rows(t)
    return pl.pallas_call(
        functools.partial(_mixer_kernel, seq, alpha),
        grid=(n_tiles,),
        in_specs=[pl.BlockSpec((t, BRANCH_W), row)] + halo_specs(3 * BRANCH_W) + halo_specs(2 * BRANCH_W)
        + halo_specs(BRANCH_W) + [pl.BlockSpec((t, N_BRANCHES * d), row), pl.BlockSpec((t, d), row)]
        + [_const_spec(c.shape) for c in consts],
        out_specs=[pl.BlockSpec((t, d), row), pl.BlockSpec((t, LANES), row),
                   pl.BlockSpec((t, LANES), row), pl.BlockSpec((None, ROW_GROUP, LANES), lambda i: (i, 0, 0)),
                   pl.BlockSpec((zr, d), row)],
        out_shape=[jax.ShapeDtypeStruct((n, d), F32), lane_out(F32), lane_out(I32),
                   jax.ShapeDtypeStruct((n_tiles, ROW_GROUP, LANES), I32),
                   jax.ShapeDtypeStruct((n_tiles * zr, d), F32)],
        scratch_shapes=[pltpu.VMEM((t + 2 * HALO, BRANCH_W), F32)] * 3
        + [pltpu.VMEM((ROW_GROUP, t + 2 * HALO - ROW_GROUP, BRANCH_W), F32)],
        compiler_params=_cparams("parallel"),
        name="mixer",
    )(o, zsc, zsc, zsc, zcf, zcf, zcf, zpl, zpl, zpl, zg, h, *consts)


def _group_copy(src_hbm, group, dst, dst_group, sem):
    return pltpu.make_async_copy(src_hbm.at[pl.ds(pl.multiple_of(group * ROW_GROUP, ROW_GROUP), ROW_GROUP), :],
                                 dst.at[pl.ds(dst_group * ROW_GROUP, ROW_GROUP), :], sem)


def _pack_bf16_pairs(y):
    c = y.shape[1] // 2
    bits = lax.bitcast_convert_type(y.astype(BF16).astype(F32), U32)
    return (bits[:, :c] >> 16) | bits[:, c:]


def _unpack_bf16_pairs(words):
    low = lax.bitcast_convert_type(words << 16, F32)
    high = lax.bitcast_convert_type(words & jnp.uint32(0xFFFF0000), F32)
    return low.astype(BF16), high.astype(BF16)


def _cast_rows(src_ref, dst_ref, chunk):
    def body(r, c):
        rows = pl.ds(pl.multiple_of(r * chunk, chunk), chunk)
        dst_ref[rows, :] = src_ref[rows, :].astype(dst_ref.dtype)
        return c
    lax.fori_loop(0, src_ref.shape[0] // chunk, body, 0)


def _expert_kernel(blk_e_ref, n_used_ref, src_ref, zt_hbm, wgu32_ref, bgu_ref, wd32_ref, bd_ref, y_ref,
                   xbuf, wgu_ref, wd_ref, sem):
    b = pl.program_id(0)
    n_used = n_used_ref[0]
    groups = MOE_BLOCK // ROW_GROUP

    @pl.when((b < n_used) & ((b == 0) | (blk_e_ref[b] != blk_e_ref[jnp.maximum(b - 1, 0)])))
    def _():
        _cast_rows(wgu32_ref, wgu_ref, LANES)
        _cast_rows(wd32_ref, wd_ref, LANES)

    def gather(block, slot):
        block = jnp.minimum(block, n_used - 1)
        for g in range(groups):
            _group_copy(zt_hbm, src_ref[block * groups + g], xbuf.at[slot], g, sem.at[slot]).start()

    def wait(slot):
        pltpu.make_async_copy(xbuf.at[slot], xbuf.at[slot], sem.at[slot]).wait()

    @pl.when(b == 0)
    def _():
        gather(0, 0)
        gather(1, 1)

    @pl.when(b < n_used)
    def _():
        wait(b % PREFETCH_DEPTH)
        f = wd_ref.shape[0]
        x = xbuf[b % PREFETCH_DEPTH].astype(BF16)
        hgu = jnp.dot(x, wgu_ref[...], preferred_element_type=F32) + bgu_ref[...]
        glu = jnp.minimum(hgu[:, :f], SWIGLU_LIMIT)
        lin = jnp.clip(hgu[:, f:], -SWIGLU_LIMIT, SWIGLU_LIMIT)
        act = glu * _sigmoid(SWIGLU_ALPHA * glu) * (lin + 1.0)
        y = jnp.dot(act.astype(BF16), wd_ref[...], preferred_element_type=F32) + bd_ref[...]
        y_ref[...] = _pack_bf16_pairs(y)
        gather(b + 2, (b + 2) % PREFETCH_DEPTH)

    @pl.when(b == n_used - 1)
    def _():
        wait((b + 1) % PREFETCH_DEPTH)
        wait((b + 2) % PREFETCH_DEPTH)

    @pl.when(b >= n_used)
    def _():
        y_ref[...] = jnp.zeros(y_ref.shape, y_ref.dtype)


def _experts(zt, blk_e, n_used, src_groups, n_blocks, layer, wgu, bgu, wd, bd):
    d = zt.shape[1]
    f2 = wgu.shape[3]
    expert = lambda b, be, nu, sg: (layer, be[b], 0, 0)
    return pl.pallas_call(
        _expert_kernel,
        grid_spec=pltpu.PrefetchScalarGridSpec(
            num_scalar_prefetch=3,
            grid=(n_blocks,),
            in_specs=[pl.BlockSpec(memory_space=pl.ANY),
                      pl.BlockSpec((None, None, d, f2), expert), pl.BlockSpec((None, None, 1, f2), expert),
                      pl.BlockSpec((None, None, f2 // 2, d), expert), pl.BlockSpec((None, None, 1, d), expert)],
            out_specs=pl.BlockSpec((MOE_BLOCK, d // 2), lambda b, be, nu, sg: (b, 0)),
            scratch_shapes=[pltpu.VMEM((PREFETCH_DEPTH, MOE_BLOCK, d), F32), pltpu.VMEM((d, f2), BF16),
                            pltpu.VMEM((f2 // 2, d), BF16), pltpu.SemaphoreType.DMA((PREFETCH_DEPTH,))],
        ),
        out_shape=jax.ShapeDtypeStruct((n_blocks * MOE_BLOCK, d // 2), U32),
        compiler_params=_cparams("arbitrary"),
        name="moe_experts",
    )(blk_e, n_used, src_groups, zt, wgu, bgu, wd, bd)


def _combine_kernel(alpha, src_ref, y_hbm, rw_ref, rpos_ref, h1_ref, g_ref, b_ref, h_ref, buf, sem):
    i = pl.program_id(0)
    n_steps = pl.num_programs(0)
    zr = buf.shape[1]
    groups = zr // ROW_GROUP
    t = h1_ref.shape[0]

    def gather(tile, slot):
        tile = jnp.minimum(tile, n_steps - 1)
        for g in range(groups):
            _group_copy(y_hbm, src_ref[tile * groups + g], buf.at[slot], g, sem.at[slot]).start(priority=g % 2)

    def wait(slot):
        pltpu.make_async_copy(buf.at[slot], buf.at[slot], sem.at[slot]).wait()

    @pl.when(i == 0)
    def _():
        gather(0, 0)
        gather(1, 1)

    slot = i % PREFETCH_DEPTH
    wait(slot)
    lane = lax.broadcasted_iota(I32, (t, LANES), 1)
    z_row = lax.broadcasted_iota(I32, (t, zr), 1)
    rw = rw_ref[...]
    rpos = rpos_ref[...]
    weights = jnp.zeros((t, zr), F32)
    for k in range(TOP_K):
        w_k = jnp.sum(jnp.where(lane == k, rw, 0.0), axis=-1, keepdims=True)
        pos_k = jnp.sum(jnp.where(lane == k, rpos, 0), axis=-1, keepdims=True)
        weights = jnp.where(z_row == pos_k, w_k, weights)
    wb = weights.astype(BF16)
    y_low, y_high = _unpack_bf16_pairs(buf[slot])
    ffn = jnp.concatenate([jnp.dot(wb, y_low, preferred_element_type=F32),
                           jnp.dot(wb, y_high, preferred_element_type=F32)], axis=1)
    h_ref[...] = _layer_norm_rows(alpha * h1_ref[...] + ffn, g_ref[...], b_ref[...])
    gather(i + 2, (i + 2) % PREFETCH_DEPTH)

    @pl.when(i == n_steps - 1)
    def _():
        wait((i + 1) % PREFETCH_DEPTH)
        wait((i + 2) % PREFETCH_DEPTH)


def _combine(y, src_groups, rw, rpos, h1, g, b, alpha, tile):
    n, d = h1.shape
    zr = _tile_sorted_rows(tile)
    row = lambda i, sg: (i, 0)
    const = lambda i, sg: (0, 0)
    return pl.pallas_call(
        functools.partial(_combine_kernel, alpha),
        grid_spec=pltpu.PrefetchScalarGridSpec(
            num_scalar_prefetch=1,
            grid=(n // tile,),
            in_specs=[pl.BlockSpec(memory_space=pl.ANY), pl.BlockSpec((tile, LANES), row),
                      pl.BlockSpec((tile, LANES), row), pl.BlockSpec((tile, d), row),
                      pl.BlockSpec((1, d), const), pl.BlockSpec((1, d), const)],
            out_specs=pl.BlockSpec((tile, d), row),
            scratch_shapes=[pltpu.VMEM((PREFETCH_DEPTH, zr, d // 2), U32), pltpu.SemaphoreType.DMA((PREFETCH_DEPTH,))],
        ),
        out_shape=jax.ShapeDtypeStruct((n, d), F32),
        compiler_params=_cparams("arbitrary"),
        name="moe_combine",
    )(src_groups, y, rw, rpos, h1, g.reshape(1, d), b.reshape(1, d))


def _rope_tables(seq):
    rows = seq // GRID_W
    row = jnp.repeat(jnp.arange(rows, dtype=F32), GRID_W)
    col = jnp.tile(jnp.arange(GRID_W, dtype=F32), rows)
    inv = ROPE_THETA ** (-jnp.arange(0, ROPE_HALF, 2, dtype=F32) / ROPE_HALF)
    ang_r = row[:, None] * inv
    ang_c = col[:, None] * inv
    cr, sr, cc, sc = jnp.cos(ang_r), jnp.sin(ang_r), jnp.cos(ang_c), jnp.sin(ang_c)
    cos = jnp.concatenate([cr, cr, cc, cc], axis=-1)
    sin = jnp.concatenate([-sr, sr, -sc, sc], axis=-1)
    return jnp.tile(cos, (1, 2)), jnp.tile(sin, (1, 2))


def _pool_constants():
    offs = np.arange(POOL_TAPS) + POOL_FIRST
    mask = np.zeros((POOL_TAPS, BRANCH_W), np.float32)
    lo = np.zeros((1, BRANCH_W), np.float32)
    hi = np.zeros((1, BRANCH_W), np.float32)
    for g, win in enumerate(POOL_WINDOWS):
        sl = slice(g * POOL_GROUP, (g + 1) * POOL_GROUP)
        mask[(offs >= -(win // 2)) & (offs <= win - 1 - win // 2), sl] = 1.0
        lo[0, sl] = win // 2
        hi[0, sl] = win - 1 - win // 2
    return jnp.asarray(mask), jnp.asarray(lo), jnp.asarray(hi)


def _block_diag(blocks):
    g, a, b = blocks.shape
    out = jnp.zeros((g * a, g * b), blocks.dtype)
    for i in range(g):
        out = out.at[i * a:(i + 1) * a, i * b:(i + 1) * b].set(blocks[i])
    return out


def _moe_tables(tile_counts, tile, n_blocks):
    nt = tile_counts.shape[0]
    bg = MOE_BLOCK // ROW_GROUP
    gz = _tile_sorted_rows(tile) // ROW_GROUP
    ng = (tile_counts + ROW_GROUP - 1) // ROW_GROUP
    strip_end = jnp.cumsum(ng, axis=1)
    strip_start = strip_end - ng
    run_end = jnp.cumsum(ng, axis=0)
    run_start = run_end - ng
    total = run_end[-1]
    blocks_e = (total + bg - 1) // bg
    blk_end = jnp.cumsum(blocks_e)
    blk_start = blk_end - blocks_e
    n_used = blk_end[-1:].astype(I32)
    b_ids = jnp.arange(n_blocks, dtype=I32)
    blk_e = jnp.minimum(jnp.sum(blk_end[None, :] <= b_ids[:, None], axis=1), N_EXPERTS - 1).astype(I32)

    experts = jnp.arange(N_EXPERTS, dtype=I32)
    is_e = blk_e[:, None] == experts[None, :]
    pick = lambda tab: jnp.sum(jnp.where(is_e, tab[None, :], 0), axis=1)
    columns = lambda tab: jnp.dot(is_e.astype(F32), tab.T.astype(F32),
                                  precision=lax.Precision.HIGHEST).astype(I32)
    q = (b_ids - pick(blk_start))[:, None] * bg + jnp.arange(bg, dtype=I32)[None, :]
    starts, ends = columns(run_start), columns(run_end)
    offset = columns(jnp.arange(nt, dtype=I32)[:, None] * gz + strip_start - run_start)
    in_strip = (starts[:, None, :] <= q[:, :, None]) & (q[:, :, None] < ends[:, None, :])
    z_src = jnp.where(q < pick(total)[:, None],
                      jnp.sum(jnp.where(in_strip, offset[:, None, :], 0), axis=2) + q, 0)

    j = jnp.arange(gz, dtype=I32)[None, :]
    e_j = jnp.minimum(jnp.sum(strip_end[:, None, :] <= j[:, :, None], axis=2), N_EXPERTS - 1)
    base = blk_start[None, :] * bg + run_start - strip_start
    picked = jnp.sum(jnp.where(e_j[:, :, None] == experts[None, None, :], base[:, None, :], 0), axis=2)
    y_src = jnp.where(j < strip_end[:, -1:], picked + j, 0)
    return blk_e, n_used, z_src.reshape(-1).astype(I32), y_src.reshape(-1).astype(I32)


def kernel(x, ln_in_g, ln_in_b, w_in, b_in, q_norm_g, k_norm_g, sc_conv_w, cf_conv_w, cf_conv_b, cf_ln_g, cf_ln_b, pool_w, pool_scale, w_branch, w_out, ln1_g, ln1_b, w_router, b_router, w_gate_up, b_gate_up, w_down, b_down, ln2_g, ln2_b):
    batch, seq, d = x.shape
    depth = w_in.shape[0]
    n = batch * seq
    alpha = float((2.0 * depth) ** 0.25)
    tile = _tile(seq, MOE_TILE)
    n_blocks = pl.cdiv((n // tile) * _tile_sorted_rows(tile), MOE_BLOCK) + N_EXPERTS

    cos, sin = _rope_tables(seq)
    pl_mask, pl_lo, pl_hi = _pool_constants()
    q_scale = jnp.concatenate([jnp.full((1, 2 * LANES), HEAD_DIM ** -0.5 * np.log2(np.e), F32),
                               jnp.ones((1, LANES), F32)], axis=-1)
    splits = np.cumsum([QKV_COLS, 3 * BRANCH_W, 2 * BRANCH_W, BRANCH_W])

    h = x.reshape(n, d)
    for l in range(depth):
        col_scale = jnp.concatenate([jnp.ones((splits[3],), F32), jnp.full((w_in.shape[2] - splits[3],), -np.log2(np.e), F32)])
        wl = (w_in[l] * col_scale).astype(BF16)
        bl = (b_in[l] * col_scale).reshape(1, -1)
        w_parts = [wl[:, :splits[0]]] + [wl[:, splits[i]:splits[i + 1]] for i in range(3)] + [wl[:, splits[3]:]]
        b_parts = [bl[:, :splits[0]]] + [bl[:, splits[i]:splits[i + 1]] for i in range(3)] + [bl[:, splits[3]:]]
        gain = jnp.concatenate([jnp.tile(q_norm_g[l], N_Q_HEADS), jnp.tile(k_norm_g[l], N_KV_HEADS)]).reshape(1, -1)

        outs = _projections(h, (ln_in_g, ln_in_b) if l == 0 else None, w_parts, b_parts, gain, q_scale, cos, sin,
                            batch, seq)
        if l == 0:
            h, *outs = outs
        q, kt, v, zsc, zcf, zpl, zg = outs
        o = _attention(q, kt, v, batch, seq)

        r_pad = LANES - N_EXPERTS
        params = dict(
            sc_w=sc_conv_w[l], cf_w=cf_conv_w[l], cf_b=cf_conv_b[l].reshape(1, -1), cf_g=cf_ln_g[l].reshape(1, -1),
            cf_beta=cf_ln_b[l].reshape(1, -1), pl_mask=pl_mask, pl_lo=pl_lo, pl_hi=pl_hi,
            pl_w=_block_diag(pool_w[l]).astype(BF16), pl_s=pool_scale[l].reshape(1, -1),
            w_br=w_branch[l].astype(BF16), w_out=w_out[l].astype(BF16),
            ln1_g=ln1_g[l].reshape(1, -1), ln1_b=ln1_b[l].reshape(1, -1),
            w_r=jnp.pad(w_router[l], ((0, 0), (0, r_pad))).astype(BF16),
            b_r=jnp.pad(b_router[l].astype(F32), (0, r_pad), constant_values=NEG_BIG).reshape(1, -1),
        )
        h1, rw, rpos, cnt, zt = _mixer(o, zsc, zcf, zpl, zg, h, params, seq, alpha)

        blk_e, n_used, z_src, y_src = _moe_tables(cnt[:, 0, :N_EXPERTS], tile, n_blocks)
        y = _experts(zt, blk_e, n_used, z_src, n_blocks, l, w_gate_up, b_gate_up[:, :, None, :],
                     w_down, b_down[:, :, None, :])
        h = _combine(y, y_src, rw, rpos, h1, ln2_g[l], ln2_b[l], alpha, tile)
    return h.reshape(batch, seq, d)
```

```python
import functools

import jax
import jax.numpy as jnp
import numpy as np
from jax import lax
from jax.experimental import pallas as pl
from jax.experimental.pallas import tpu as pltpu

F32 = jnp.float32
BF16 = jnp.bfloat16
I32 = jnp.int32
U32 = jnp.uint32

GRID_W = 64
HEAD_DIM = 64
N_Q_HEADS = 4
N_KV_HEADS = 2
Q_GROUP = N_Q_HEADS // N_KV_HEADS
ROPE_HALF = HEAD_DIM // 2
ROPE_THETA = 10000.0
BRANCH_W = 256
N_BRANCHES = 4
SC_WIDTH = 3
CF_WIDTH = 31
POOL_WINDOWS = (2, 4, 8, 16)
POOL_GROUP = BRANCH_W // len(POOL_WINDOWS)
POOL_TAPS = 16
POOL_FIRST = -8
N_EXPERTS = 32
TOP_K = 4
SWIGLU_LIMIT = 7.0
SWIGLU_ALPHA = 1.702
LN_EPS = 1e-5
RMS_EPS = 1e-6
QKV_COLS = N_Q_HEADS * HEAD_DIM + 2 * N_KV_HEADS * HEAD_DIM

LANES = 128
HALO = 16
ROW_GROUP = 8
MOE_TILE = 256
MOE_BLOCK = 512
PREFETCH_DEPTH = 3
VMEM_LIMIT = 56 * 1024 * 1024
NEG_BIG = -1e30


def _tile_sorted_rows(t):
    return t * TOP_K + N_EXPERTS * ROW_GROUP


def _cparams(*sem):
    return pltpu.CompilerParams(dimension_semantics=sem, vmem_limit_bytes=VMEM_LIMIT)


def _tile(n, pref):
    t = min(n, pref)
    assert n % t == 0, (n, t)
    return t


def _const_spec(shape):
    nd = len(shape)
    return pl.BlockSpec(shape, lambda *_: (0,) * nd)


def _layer_norm_rows(y, g, b):
    mu = jnp.mean(y, axis=-1, keepdims=True)
    d = y - mu
    var = jnp.mean(d * d, axis=-1, keepdims=True)
    return d * lax.rsqrt(var + LN_EPS) * g + b


def _sigmoid(x):
    return 1.0 / (1.0 + jnp.exp(-x))


def _attn_tiles(seq):
    return _tile(seq, 1024), _tile(seq, 2048)


def _proj_kernel(apply_ln, *refs):
    refs = list(refs)
    x_ref = refs.pop(0)
    ln_refs = [refs.pop(0), refs.pop(0)] if apply_ln else None
    (wqkv_ref, bqkv_ref, gain_ref, scale_ref, cos_ref, sin_ref,
     wsc_ref, wcf_ref, wpl_ref, wg_ref, bsc_ref, bcf_ref, bpl_ref, bg_ref) = refs[:14]
    outs = refs[14:]
    h = x_ref[...]
    if apply_ln:
        h = _layer_norm_rows(h, ln_refs[0][...], ln_refs[1][...])
        outs.pop(0)[...] = h
    q_ref, kt_ref, v_ref, zsc_ref, zcf_ref, zpl_ref, zg_ref = outs
    hb = h.astype(BF16)

    z = jnp.dot(hb, wqkv_ref[...], preferred_element_type=F32) + bqkv_ref[...]
    t = z.shape[0]
    lane = lax.broadcasted_iota(I32, (t, LANES), 1)
    low_head = lane < HEAD_DIM
    first_half = (lane % ROPE_HALF) < (ROPE_HALF // 2)
    cos = cos_ref[...]
    sin = sin_ref[...]
    roped = []
    for c in range(3):
        x = z[:, c * LANES:(c + 1) * LANES]
        x2 = x * x
        s_all = jnp.sum(x2, axis=-1, keepdims=True)
        s_low = jnp.sum(jnp.where(low_head, x2, 0.0), axis=-1, keepdims=True)
        ms = jnp.where(low_head, s_low, s_all - s_low) * (1.0 / HEAD_DIM)
        xn = x * lax.rsqrt(ms + RMS_EPS) * gain_ref[:, c * LANES:(c + 1) * LANES]
        partner = jnp.where(first_half, pltpu.roll(xn, LANES - ROPE_HALF // 2, 1), pltpu.roll(xn, ROPE_HALF // 2, 1))
        roped.append((xn * cos + partner * sin) * scale_ref[:, c * LANES:(c + 1) * LANES])
    q_ref[:, 0:LANES] = roped[0].astype(BF16)
    q_ref[:, LANES:2 * LANES] = roped[1].astype(BF16)
    kt_ref[...] = roped[2].T.astype(BF16)
    v = z[:, 3 * LANES:4 * LANES]
    tail = jnp.where(lane == HEAD_DIM, 1.0, 0.0)
    v_ref[0] = jnp.where(low_head, v, tail).astype(BF16)
    v_ref[1] = jnp.where(low_head, pltpu.roll(v, HEAD_DIM, 1), tail).astype(BF16)

    for w_ref, b_ref, o_ref in ((wsc_ref, bsc_ref, zsc_ref), (wcf_ref, bcf_ref, zcf_ref),
                                (wpl_ref, bpl_ref, zpl_ref), (wg_ref, bg_ref, zg_ref)):
        o_ref[...] = (jnp.dot(hb, w_ref[...], preferred_element_type=F32) + b_ref[...]).astype(BF16)


def _projections(x, ln, ws, bs, gain, scale, cos, sin, batch, seq):
    n, d = x.shape
    t = _tile(seq, 256)
    tk = _attn_tiles(seq)[1]
    assert tk % t == 0
    per_seq, per_chunk, nc = seq // t, tk // t, seq // tk
    row = lambda i: (i, 0)
    rope = lambda i: (i % per_seq, 0)
    resident = lambda c: pl.BlockSpec((d, c), lambda i: (0, 0), pipeline_mode=pl.Buffered(1))
    widths = [w.shape[1] for w in ws]
    apply_ln = ln is not None
    ln_args = [ln[0].reshape(1, d), ln[1].reshape(1, d)] if apply_ln else []
    in_specs = ([pl.BlockSpec((t, d), row)] + [_const_spec((1, d))] * len(ln_args)
                + [resident(widths[0]), _const_spec((1, widths[0])), _const_spec((1, 3 * LANES)),
                   _const_spec((1, 3 * LANES)), pl.BlockSpec((t, LANES), rope), pl.BlockSpec((t, LANES), rope)]
                + [resident(c) for c in widths[1:]] + [_const_spec((1, c)) for c in widths[1:]])
    out_specs = ([pl.BlockSpec((t, d), row)] if apply_ln else []) + [
        pl.BlockSpec((t, 2 * LANES), row),
        pl.BlockSpec((None, None, LANES, t), lambda i: (i // per_seq, (i % per_seq) // per_chunk, 0, i % per_chunk)),
        pl.BlockSpec((None, None, N_KV_HEADS, t, LANES),
                     lambda i: (i // per_seq, (i % per_seq) // per_chunk, 0, i % per_chunk, 0)),
    ] + [pl.BlockSpec((t, c), row) for c in widths[1:]]
    out_shape = ([jax.ShapeDtypeStruct((n, d), F32)] if apply_ln else []) + [
        jax.ShapeDtypeStruct((n, 2 * LANES), BF16),
        jax.ShapeDtypeStruct((batch, nc, LANES, tk), BF16),
        jax.ShapeDtypeStruct((batch, nc, N_KV_HEADS, tk, LANES), BF16),
    ] + [jax.ShapeDtypeStruct((n, c), BF16) for c in widths[1:]]
    return pl.pallas_call(
        functools.partial(_proj_kernel, apply_ln),
        grid=(n // t,),
        in_specs=in_specs, out_specs=out_specs, out_shape=out_shape,
        compiler_params=_cparams("parallel"),
        name="projections",
    )(x, *ln_args, ws[0], bs[0], gain, scale, cos, sin, *ws[1:], *bs[1:])


def _attn_kernel(q_ref, kt_ref, v_ref, o_ref, m_scr, acc_scr):
    h = pl.program_id(1)
    tq = q_ref.shape[0]
    rows = Q_GROUP * tq
    n_chunks, _, tk = kt_ref.shape
    lane = lax.broadcasted_iota(I32, (tq, LANES), 1)
    own_half = (lane // HEAD_DIM) == h
    first = h == 0
    qf = q_ref[...].astype(F32)
    swapped = pltpu.roll(qf, HEAD_DIM, 1)
    q = jnp.concatenate([jnp.where(own_half, jnp.where(first, qf, swapped), 0.0),
                         jnp.where(own_half, jnp.where(first, swapped, qf), 0.0)], axis=0).astype(BF16)
    m_scr[...] = jnp.full(m_scr.shape, -jnp.inf, F32)
    acc_scr[...] = jnp.zeros(acc_scr.shape, F32)

    def body(c, carry):
        s = jnp.dot(q, kt_ref[c], preferred_element_type=F32)
        m_prev = m_scr[...]
        m_new = jnp.maximum(m_prev, jnp.max(s, axis=-1, keepdims=True))
        alpha = jnp.exp2(m_prev - m_new)
        p = jnp.exp2(s - jnp.concatenate([m_new] * (tk // LANES), axis=1)).astype(BF16)
        acc_scr[...] = alpha * acc_scr[...] + jnp.dot(p, v_ref[c], preferred_element_type=F32)
        m_scr[...] = m_new
        return carry

    lax.fori_loop(0, n_chunks, body, 0, unroll=2 if n_chunks % 2 == 0 else 1)
    acc = acc_scr[...]
    res = acc / acc[:, HEAD_DIM:HEAD_DIM + 1]
    o_ref[...] = jnp.where(lane < HEAD_DIM, res[:tq], pltpu.roll(res[tq:], HEAD_DIM, 1)).astype(BF16)


def _attention(q, kt, v, batch, seq):
    tq, tk = _attn_tiles(seq)
    nc = seq // tk
    per_seq = seq // tq
    rows = Q_GROUP * tq
    return pl.pallas_call(
        _attn_kernel,
        grid=(batch, N_KV_HEADS, per_seq),
        in_specs=[pl.BlockSpec((tq, LANES), lambda b, h, i: (b * per_seq + i, h)),
                  pl.BlockSpec((None, nc, LANES, tk), lambda b, h, i: (b, 0, 0, 0)),
                  pl.BlockSpec((None, nc, None, tk, LANES), lambda b, h, i: (b, 0, h, 0, 0))],
        out_specs=pl.BlockSpec((None, tq, LANES), lambda b, h, i: (b, i, h)),
        out_shape=jax.ShapeDtypeStruct((batch, seq, N_KV_HEADS * LANES), BF16),
        scratch_shapes=[pltpu.VMEM((rows, LANES), F32), pltpu.VMEM((rows, LANES), F32)],
        compiler_params=_cparams("parallel", "parallel", "arbitrary"),
        name="attention",
    )(q, kt, v).reshape(batch * seq, N_KV_HEADS * LANES)


def _dwconv(xs_ref, w_ref, n_taps, first_row, rows, shifted_ref=None):
    if shifted_ref is None:
        taps = [xs_ref[pl.ds(first_row + j, rows), :] for j in range(n_taps)]
    else:
        span = shifted_ref.shape[1]
        assert first_row + n_taps - 1 + rows <= span + ROW_GROUP - 1
        shifts = sorted({(first_row + j) % ROW_GROUP for j in range(n_taps)} - {0})
        for s in shifts:
            shifted_ref[s] = xs_ref[pl.ds(s, span), :]
        taps = []
        for j in range(n_taps):
            s, base = (first_row + j) % ROW_GROUP, (first_row + j) // ROW_GROUP * ROW_GROUP
            src = xs_ref if s == 0 else shifted_ref.at[s]
            taps.append(src[pl.ds(base, rows), :])
    acc = None
    for j, tap in enumerate(taps):
        term = tap * w_ref[j:j + 1, :]
        acc = term if acc is None else acc + term
    return acc


def _mixer_kernel(seq, alpha,
                  o_ref, scp_ref, scc_ref, scn_ref, cfp_ref, cfc_ref, cfn_ref, plp_ref, plc_ref, pln_ref,
                  zg_ref, h_ref, scw_ref, cfw_ref, cfb_ref, cfg_ref, cfbeta_ref, plmask_ref, pllo_ref, plhi_ref,
                  plw_ref, pls_ref, wbr_ref, wout_ref, ln1g_ref, ln1b_ref, wr_ref, br_ref,
                  h1_ref, rw_ref, rpos_ref, cnt_ref, zt_ref,
                  xsc, xcf, xpl, xsh):
    t = h_ref.shape[0]
    w = BRANCH_W
    i = pl.program_id(0)
    per_seq = seq // t
    j = i % per_seq
    has_prev = (j > 0).astype(F32)
    has_next = (j < per_seq - 1).astype(F32)

    def sc_v(z):
        z = z.astype(F32)
        return z[:, w:2 * w] * z[:, 2 * w:3 * w]
    xsc[0:HALO, :] = sc_v(scp_ref[...]) * has_prev
    xsc[HALO:HALO + t, :] = sc_v(scc_ref[...])
    xsc[HALO + t:2 * HALO + t, :] = sc_v(scn_ref[...]) * has_next
    y_sc = scc_ref[:, 0:w].astype(F32) * _dwconv(xsc, scw_ref, SC_WIDTH, HALO - SC_WIDTH // 2, t)

    def cf_v(z):
        z = z.astype(F32)
        return z[:, 0:w] * _sigmoid(z[:, w:2 * w])
    xcf[0:HALO, :] = cf_v(cfp_ref[...]) * has_prev
    xcf[HALO:HALO + t, :] = cf_v(cfc_ref[...])
    xcf[HALO + t:2 * HALO + t, :] = cf_v(cfn_ref[...]) * has_next
    y_cf = _dwconv(xcf, cfw_ref, CF_WIDTH, HALO - CF_WIDTH // 2, t, xsh) + cfb_ref[...]
    y_cf = _layer_norm_rows(y_cf, cfg_ref[...], cfbeta_ref[...])
    y_cf = y_cf * _sigmoid(y_cf)

    u = plc_ref[...].astype(F32)
    xpl[0:HALO, :] = plp_ref[...].astype(F32) * has_prev
    xpl[HALO:HALO + t, :] = u
    xpl[HALO + t:2 * HALO + t, :] = pln_ref[...].astype(F32) * has_next
    win_sum = _dwconv(xpl, plmask_ref, POOL_TAPS, HALO + POOL_FIRST, t, xsh)
    pos = (j * t + lax.broadcasted_iota(I32, (t, w), 0)).astype(F32)
    n_win = jnp.minimum(pos + plhi_ref[...], float(seq - 1)) - jnp.maximum(pos - pllo_ref[...], 0.0) + 1.0
    y_pl = win_sum / n_win - u
    y_pl = jnp.dot(y_pl.astype(BF16), plw_ref[...], preferred_element_type=F32) * pls_ref[...]

    merged = None
    for g, br in enumerate((o_ref[...], y_sc.astype(BF16), y_cf.astype(BF16), y_pl.astype(BF16))):
        proj = jnp.dot(br, wbr_ref[g], preferred_element_type=F32)
        gate = 1.0 / (1.0 + jnp.exp2(zg_ref[:, g * proj.shape[1]:(g + 1) * proj.shape[1]].astype(F32)))
        merged = gate * proj if merged is None else merged + gate * proj
    mix = jnp.dot(merged.astype(BF16), wout_ref[...], preferred_element_type=F32)
    h1 = _layer_norm_rows(alpha * h_ref[...] + mix, ln1g_ref[...], ln1b_ref[...])
    h1_ref[...] = h1

    logits = jnp.dot(h1.astype(BF16), wr_ref[...], preferred_element_type=F32) + br_ref[...]
    lane = lax.broadcasted_iota(I32, (t, LANES), 1)
    vals = logits
    tops, sels = [], []
    for _ in range(TOP_K):
        m = jnp.max(vals, axis=-1, keepdims=True)
        first = jnp.min(jnp.where(vals == m, lane, LANES), axis=-1, keepdims=True)
        sel = lane == first
        vals = jnp.where(sel, -jnp.inf, vals)
        tops.append(m)
        sels.append(sel)
    exps = [jnp.exp(m - tops[0]) for m in tops]
    denom = exps[0] + exps[1] + exps[2] + exps[3]

    onehot = jnp.where(sels[0] | sels[1] | sels[2] | sels[3], 1.0, 0.0)
    r_id = lax.broadcasted_iota(I32, (t, t), 0)
    c_id = lax.broadcasted_iota(I32, (t, t), 1)
    lower = jnp.where(c_id < r_id, 1.0, 0.0).astype(BF16)
    rank_e = jnp.dot(lower, onehot.astype(BF16), preferred_element_type=F32)
    cnt = jnp.sum(onehot, axis=0, keepdims=True)
    groups = jnp.floor((cnt + (ROW_GROUP - 1.0)) * (1.0 / ROW_GROUP))
    e_r = lax.broadcasted_iota(I32, (LANES, LANES), 0)
    e_c = lax.broadcasted_iota(I32, (LANES, LANES), 1)
    before = jnp.where(e_r < e_c, 1.0, 0.0).astype(BF16)
    strip_start = jnp.dot(jnp.broadcast_to(groups, (ROW_GROUP, LANES)).astype(BF16), before,
                          preferred_element_type=F32)[0:1, :]
    row_e = rank_e + ROW_GROUP * strip_start
    cnt_ref[...] = jnp.broadcast_to(cnt, cnt_ref.shape).astype(I32)

    zr = zt_ref.shape[0]
    z_row = lax.broadcasted_iota(I32, (t, zr), 1)
    rw = jnp.zeros((t, LANES), F32)
    rpos = jnp.zeros((t, LANES), I32)
    scatter = jnp.zeros((t, zr), F32)
    for k in range(TOP_K):
        pos_k = jnp.sum(jnp.where(sels[k], row_e, 0.0), axis=-1, keepdims=True).astype(I32)
        scatter = jnp.where(z_row == pos_k, 1.0, scatter)
        rw = jnp.where(lane == k, exps[k] / denom, rw)
        rpos = jnp.where(lane == k, pos_k, rpos)
    rw_ref[...] = rw
    rpos_ref[...] = rpos
    zt_ref[...] = lax.dot_general(scatter.astype(BF16), h1.astype(BF16), (((0,), (0,)), ((), ())),
                                  preferred_element_type=F32)


def _mixer(o, zsc, zcf, zpl, zg, h, p, seq, alpha):
    n, d = h.shape
    t = _tile(seq, MOE_TILE)
    hb = t // HALO
    n_halo = n // HALO
    row = lambda i: (i, 0)
    prev = lambda i: (jnp.maximum(i * hb - 1, 0), 0)
    nxt = lambda i: (jnp.minimum((i + 1) * hb, n_halo - 1), 0)

    def halo_specs(c):
        return [pl.BlockSpec((HALO, c), prev), pl.BlockSpec((t, c), row), pl.BlockSpec((HALO, c), nxt)]

    consts = [p["sc_w"], p["cf_w"], p["cf_b"], p["cf_g"], p["cf_beta"], p["pl_mask"], p["pl_lo"], p["pl_hi"],
              p["pl_w"], p["pl_s"], p["w_br"], p["w_out"], p["ln1_g"], p["ln1_b"], p["w_r"], p["b_r"]]
    lane_out = lambda dt: jax.ShapeDtypeStruct((n, LANES), dt)
    n_tiles = n // t
    zr = _tile_sorted_rows(t)
    return pl.pallas_call(
        functools.partial(_mixer_kernel, seq, alpha),
        grid=(n_tiles,),
        in_specs=[pl.BlockSpec((t, BRANCH_W), row)] + halo_specs(3 * BRANCH_W) + halo_specs(2 * BRANCH_W)
        + halo_specs(BRANCH_W) + [pl.BlockSpec((t, N_BRANCHES * d), row), pl.BlockSpec((t, d), row)]
        + [_const_spec(c.shape) for c in consts],
        out_specs=[pl.BlockSpec((t, d), row), pl.BlockSpec((t, LANES), row),
                   pl.BlockSpec((t, LANES), row), pl.BlockSpec((None, ROW_GROUP, LANES), lambda i: (i, 0, 0)),
                   pl.BlockSpec((zr, d), row)],
        out_shape=[jax.ShapeDtypeStruct((n, d), F32), lane_out(F32), lane_out(I32),
                   jax.ShapeDtypeStruct((n_tiles, ROW_GROUP, LANES), I32),
                   jax.ShapeDtypeStruct((n_tiles * zr, d), F32)],
        scratch_shapes=[pltpu.VMEM((t + 2 * HALO, BRANCH_W), F32)] * 3
        + [pltpu.VMEM((ROW_GROUP, t + 2 * HALO - ROW_GROUP, BRANCH_W), F32)],
        compiler_params=_cparams("parallel"),
        name="mixer",
    )(o, zsc, zsc, zsc, zcf, zcf, zcf, zpl, zpl, zpl, zg, h, *consts)


def _group_copy(src_hbm, group, dst, dst_group, sem):
    return pltpu.make_async_copy(src_hbm.at[pl.ds(pl.multiple_of(group * ROW_GROUP, ROW_GROUP), ROW_GROUP), :],
                                 dst.at[pl.ds(dst_group * ROW_GROUP, ROW_GROUP), :], sem)


def _pack_bf16_pairs(y):
    c = y.shape[1] // 2
    bits = lax.bitcast_convert_type(y.astype(BF16).astype(F32), U32)
    return (bits[:, :c] >> 16) | bits[:, c:]


def _unpack_bf16_pairs(words):
    low = lax.bitcast_convert_type(words << 16, F32)
    high = lax.bitcast_convert_type(words & jnp.uint32(0xFFFF0000), F32)
    return low.astype(BF16), high.astype(BF16)


def _cast_rows(src_ref, dst_ref, chunk):
    def body(r, c):
        rows = pl.ds(pl.multiple_of(r * chunk, chunk), chunk)
        dst_ref[rows, :] = src_ref[rows, :].astype(dst_ref.dtype)
        return c
    lax.fori_loop(0, src_ref.shape[0] // chunk, body, 0)


def _expert_kernel(blk_e_ref, n_used_ref, src_ref, zt_hbm, wgu32_ref, bgu_ref, wd32_ref, bd_ref, y_ref,
                   xbuf, wgu_ref, wd_ref, sem):
    b = pl.program_id(0)
    n_used = n_used_ref[0]
    groups = MOE_BLOCK // ROW_GROUP

    @pl.when((b < n_used) & ((b == 0) | (blk_e_ref[b] != blk_e_ref[jnp.maximum(b - 1, 0)])))
    def _():
        _cast_rows(wgu32_ref, wgu_ref, LANES)
        _cast_rows(wd32_ref, wd_ref, LANES)

    def gather(block, slot, part=0, n_parts=1):
        block = jnp.minimum(block, n_used - 1)
        for g in range(part * groups // n_parts, (part + 1) * groups // n_parts):
            _group_copy(zt_hbm, src_ref[block * groups + g], xbuf.at[slot], g, sem.at[slot]).start()

    def wait(slot):
        pltpu.make_async_copy(xbuf.at[slot], xbuf.at[slot], sem.at[slot]).wait()

    @pl.when(b == 0)
    def _():
        gather(0, 0)
        gather(1, 1)

    @pl.when(b < n_used)
    def _():
        ahead = functools.partial(gather, b + 2, (b + 2) % PREFETCH_DEPTH, n_parts=4)
        wait(b % PREFETCH_DEPTH)
        f = wd_ref.shape[0]
        x = xbuf[b % PREFETCH_DEPTH].astype(BF16)
        ahead(part=0)
        glu = jnp.dot(x, wgu_ref[:, :f], preferred_element_type=F32) + bgu_ref[:, :f]
        ahead(part=1)
        lin = jnp.dot(x, wgu_ref[:, f:], preferred_element_type=F32) + bgu_ref[:, f:]
        glu = jnp.minimum(glu, SWIGLU_LIMIT)
        lin = jnp.clip(lin, -SWIGLU_LIMIT, SWIGLU_LIMIT)
        act = glu * _sigmoid(SWIGLU_ALPHA * glu) * (lin + 1.0)
        ahead(part=2)
        y = jnp.dot(act.astype(BF16), wd_ref[...], preferred_element_type=F32) + bd_ref[...]
        y_ref[...] = _pack_bf16_pairs(y)
        ahead(part=3)

    @pl.when(b == n_used - 1)
    def _():
        wait((b + 1) % PREFETCH_DEPTH)
        wait((b + 2) % PREFETCH_DEPTH)

    @pl.when(b >= n_used)
    def _():
        y_ref[...] = jnp.zeros(y_ref.shape, y_ref.dtype)


def _experts(zt, blk_e, n_used, src_groups, n_blocks, layer, wgu, bgu, wd, bd):
    d = zt.shape[1]
    f2 = wgu.shape[3]
    expert = lambda b, be, nu, sg: (layer, be[b], 0, 0)
    return pl.pallas_call(
        _expert_kernel,
        grid_spec=pltpu.PrefetchScalarGridSpec(
            num_scalar_prefetch=3,
            grid=(n_blocks,),
            in_specs=[pl.BlockSpec(memory_space=pl.ANY),
                      pl.BlockSpec((None, None, d, f2), expert), pl.BlockSpec((None, None, 1, f2), expert),
                      pl.BlockSpec((None, None, f2 // 2, d), expert), pl.BlockSpec((None, None, 1, d), expert)],
            out_specs=pl.BlockSpec((MOE_BLOCK, d // 2), lambda b, be, nu, sg: (b, 0)),
            scratch_shapes=[pltpu.VMEM((PREFETCH_DEPTH, MOE_BLOCK, d), F32), pltpu.VMEM((d, f2), BF16),
                            pltpu.VMEM((f2 // 2, d), BF16), pltpu.SemaphoreType.DMA((PREFETCH_DEPTH,))],
        ),
        out_shape=jax.ShapeDtypeStruct((n_blocks * MOE_BLOCK, d // 2), U32),
        compiler_params=_cparams("arbitrary"),
        name="moe_experts",
    )(blk_e, n_used, src_groups, zt, wgu, bgu, wd, bd)


def _combine_kernel(alpha, src_ref, y_hbm, rw_ref, rpos_ref, h1_ref, g_ref, b_ref, h_ref, buf, sem):
    i = pl.program_id(0)
    n_steps = pl.num_programs(0)
    zr = buf.shape[1]
    groups = zr // ROW_GROUP
    t = h1_ref.shape[0]

    def gather(tile, slot, part=0, n_parts=1):
        tile = jnp.minimum(tile, n_steps - 1)
        for g in range(part * groups // n_parts, (part + 1) * groups // n_parts):
            _group_copy(y_hbm, src_ref[tile * groups + g], buf.at[slot], g, sem.at[slot]).start(priority=g % 2)

    def wait(slot):
        pltpu.make_async_copy(buf.at[slot], buf.at[slot], sem.at[slot]).wait()

    @pl.when(i == 0)
    def _():
        gather(0, 0)
        gather(1, 1)

    n_parts = TOP_K + 4
    ahead = functools.partial(gather, i + 2, (i + 2) % PREFETCH_DEPTH, n_parts=n_parts)
    slot = i % PREFETCH_DEPTH
    wait(slot)
    lane = lax.broadcasted_iota(I32, (t, LANES), 1)
    z_row = lax.broadcasted_iota(I32, (t, zr), 1)
    rw = rw_ref[...]
    rpos = rpos_ref[...]
    weights = jnp.zeros((t, zr), F32)
    for k in range(TOP_K):
        w_k = jnp.sum(jnp.where(lane == k, rw, 0.0), axis=-1, keepdims=True)
        pos_k = jnp.sum(jnp.where(lane == k, rpos, 0), axis=-1, keepdims=True)
        weights = jnp.where(z_row == pos_k, w_k, weights)
        ahead(part=k)
    wb = weights.astype(BF16)
    y_low, y_high = _unpack_bf16_pairs(buf[slot])
    ahead(part=TOP_K)
    ffn_low = jnp.dot(wb, y_low, preferred_element_type=F32)
    ahead(part=TOP_K + 1)
    ffn_high = jnp.dot(wb, y_high, preferred_element_type=F32)
    ahead(part=TOP_K + 2)
    ffn = jnp.concatenate([ffn_low, ffn_high], axis=1)
    h_ref[...] = _layer_norm_rows(alpha * h1_ref[...] + ffn, g_ref[...], b_ref[...])
    ahead(part=TOP_K + 3)

    @pl.when(i == n_steps - 1)
    def _():
        wait((i + 1) % PREFETCH_DEPTH)
        wait((i + 2) % PREFETCH_DEPTH)


def _combine(y, src_groups, rw, rpos, h1, g, b, alpha, tile):
    n, d = h1.shape
    zr = _tile_sorted_rows(tile)
    row = lambda i, sg: (i, 0)
    const = lambda i, sg: (0, 0)
    return pl.pallas_call(
        functools.partial(_combine_kernel, alpha),
        grid_spec=pltpu.PrefetchScalarGridSpec(
            num_scalar_prefetch=1,
            grid=(n // tile,),
            in_specs=[pl.BlockSpec(memory_space=pl.ANY), pl.BlockSpec((tile, LANES), row),
                      pl.BlockSpec((tile, LANES), row), pl.BlockSpec((tile, d), row),
                      pl.BlockSpec((1, d), const), pl.BlockSpec((1, d), const)],
            out_specs=pl.BlockSpec((tile, d), row),
            scratch_shapes=[pltpu.VMEM((PREFETCH_DEPTH, zr, d // 2), U32), pltpu.SemaphoreType.DMA((PREFETCH_DEPTH,))],
        ),
        out_shape=jax.ShapeDtypeStruct((n, d), F32),
        compiler_params=_cparams("arbitrary"),
        name="moe_combine",
    )(src_groups, y, rw, rpos, h1, g.reshape(1, d), b.reshape(1, d))


def _rope_tables(seq):
    rows = seq // GRID_W
    row = jnp.repeat(jnp.arange(rows, dtype=F32), GRID_W)
    col = jnp.tile(jnp.arange(GRID_W, dtype=F32), rows)
    inv = ROPE_THETA ** (-jnp.arange(0, ROPE_HALF, 2, dtype=F32) / ROPE_HALF)
    ang_r = row[:, None] * inv
    ang_c = col[:, None] * inv
    cr, sr, cc, sc = jnp.cos(ang_r), jnp.sin(ang_r), jnp.cos(ang_c), jnp.sin(ang_c)
    cos = jnp.concatenate([cr, cr, cc, cc], axis=-1)
    sin = jnp.concatenate([-sr, sr, -sc, sc], axis=-1)
    return jnp.tile(cos, (1, 2)), jnp.tile(sin, (1, 2))


def _pool_constants():
    offs = np.arange(POOL_TAPS) + POOL_FIRST
    mask = np.zeros((POOL_TAPS, BRANCH_W), np.float32)
    lo = np.zeros((1, BRANCH_W), np.float32)
    hi = np.zeros((1, BRANCH_W), np.float32)
    for g, win in enumerate(POOL_WINDOWS):
        sl = slice(g * POOL_GROUP, (g + 1) * POOL_GROUP)
        mask[(offs >= -(win // 2)) & (offs <= win - 1 - win // 2), sl] = 1.0
        lo[0, sl] = win // 2
        hi[0, sl] = win - 1 - win // 2
    return jnp.asarray(mask), jnp.asarray(lo), jnp.asarray(hi)


def _block_diag(blocks):
    g, a, b = blocks.shape
    out = jnp.zeros((g * a, g * b), blocks.dtype)
    for i in range(g):
        out = out.at[i * a:(i + 1) * a, i * b:(i + 1) * b].set(blocks[i])
    return out


def _moe_tables(tile_counts, tile, n_blocks):
    nt = tile_counts.shape[0]
    bg = MOE_BLOCK // ROW_GROUP
    gz = _tile_sorted_rows(tile) // ROW_GROUP
    ng = (tile_counts + ROW_GROUP - 1) // ROW_GROUP
    strip_end = jnp.cumsum(ng, axis=1)
    strip_start = strip_end - ng
    run_end = jnp.cumsum(ng, axis=0)
    run_start = run_end - ng
    total = run_end[-1]
    blocks_e = (total + bg - 1) // bg
    blk_end = jnp.cumsum(blocks_e)
    blk_start = blk_end - blocks_e
    n_used = blk_end[-1:].astype(I32)
    b_ids = jnp.arange(n_blocks, dtype=I32)
    blk_e = jnp.minimum(jnp.sum(blk_end[None, :] <= b_ids[:, None], axis=1), N_EXPERTS - 1).astype(I32)

    experts = jnp.arange(N_EXPERTS, dtype=I32)
    is_e = blk_e[:, None] == experts[None, :]
    pick = lambda tab: jnp.sum(jnp.where(is_e, tab[None, :], 0), axis=1)
    columns = lambda tab: jnp.dot(is_e.astype(F32), tab.T.astype(F32),
                                  precision=lax.Precision.HIGHEST).astype(I32)
    q = (b_ids - pick(blk_start))[:, None] * bg + jnp.arange(bg, dtype=I32)[None, :]
    starts, ends = columns(run_start), columns(run_end)
    offset = columns(jnp.arange(nt, dtype=I32)[:, None] * gz + strip_start - run_start)
    in_strip = (starts[:, None, :] <= q[:, :, None]) & (q[:, :, None] < ends[:, None, :])
    z_src = jnp.where(q < pick(total)[:, None],
                      jnp.sum(jnp.where(in_strip, offset[:, None, :], 0), axis=2) + q, 0)

    j = jnp.arange(gz, dtype=I32)[None, :]
    e_j = jnp.minimum(jnp.sum(strip_end[:, None, :] <= j[:, :, None], axis=2), N_EXPERTS - 1)
    base = blk_start[None, :] * bg + run_start - strip_start
    picked = jnp.sum(jnp.where(e_j[:, :, None] == experts[None, None, :], base[:, None, :], 0), axis=2)
    y_src = jnp.where(j < strip_end[:, -1:], picked + j, 0)
    return blk_e, n_used, z_src.reshape(-1).astype(I32), y_src.reshape(-1).astype(I32)


def kernel(x, ln_in_g, ln_in_b, w_in, b_in, q_norm_g, k_norm_g, sc_conv_w, cf_conv_w, cf_conv_b, cf_ln_g, cf_ln_b, pool_w, pool_scale, w_branch, w_out, ln1_g, ln1_b, w_router, b_router, w_gate_up, b_gate_up, w_down, b_down, ln2_g, ln2_b):
    batch, seq, d = x.shape
    depth = w_in.shape[0]
    n = batch * seq
    alpha = float((2.0 * depth) ** 0.25)
    tile = _tile(seq, MOE_TILE)
    n_blocks = pl.cdiv((n // tile) * _tile_sorted_rows(tile), MOE_BLOCK) + N_EXPERTS

    cos, sin = _rope_tables(seq)
    pl_mask, pl_lo, pl_hi = _pool_constants()
    q_scale = jnp.concatenate([jnp.full((1, 2 * LANES), HEAD_DIM ** -0.5 * np.log2(np.e), F32),
                               jnp.ones((1, LANES), F32)], axis=-1)
    splits = np.cumsum([QKV_COLS, 3 * BRANCH_W, 2 * BRANCH_W, BRANCH_W])

    h = x.reshape(n, d)
    for l in range(depth):
        col_scale = jnp.concatenate([jnp.ones((splits[3],), F32), jnp.full((w_in.shape[2] - splits[3],), -np.log2(np.e), F32)])
        wl = (w_in[l] * col_scale).astype(BF16)
        bl = (b_in[l] * col_scale).reshape(1, -1)
        w_parts = [wl[:, :splits[0]]] + [wl[:, splits[i]:splits[i + 1]] for i in range(3)] + [wl[:, splits[3]:]]
        b_parts = [bl[:, :splits[0]]] + [bl[:, splits[i]:splits[i + 1]] for i in range(3)] + [bl[:, splits[3]:]]
        gain = jnp.concatenate([jnp.tile(q_norm_g[l], N_Q_HEADS), jnp.tile(k_norm_g[l], N_KV_HEADS)]).reshape(1, -1)

        outs = _projections(h, (ln_in_g, ln_in_b) if l == 0 else None, w_parts, b_parts, gain, q_scale, cos, sin,
                            batch, seq)
        if l == 0:
            h, *outs = outs
        q, kt, v, zsc, zcf, zpl, zg = outs
        o = _attention(q, kt, v, batch, seq)

        r_pad = LANES - N_EXPERTS
        params = dict(
            sc_w=sc_conv_w[l], cf_w=cf_conv_w[l], cf_b=cf_conv_b[l].reshape(1, -1), cf_g=cf_ln_g[l].reshape(1, -1),
            cf_beta=cf_ln_b[l].reshape(1, -1), pl_mask=pl_mask, pl_lo=pl_lo, pl_hi=pl_hi,
            pl_w=_block_diag(pool_w[l]).astype(BF16), pl_s=pool_scale[l].reshape(1, -1),
            w_br=w_branch[l].astype(BF16), w_out=w_out[l].astype(BF16),
            ln1_g=ln1_g[l].reshape(1, -1), ln1_b=ln1_b[l].reshape(1, -1),
            w_r=jnp.pad(w_router[l], ((0, 0), (0, r_pad))).astype(BF16),
            b_r=jnp.pad(b_router[l].astype(F32), (0, r_pad), constant_values=NEG_BIG).reshape(1, -1),
        )
        h1, rw, rpos, cnt, zt = _mixer(o, zsc, zcf, zpl, zg, h, params, seq, alpha)

        blk_e, n_used, z_src, y_src = _moe_tables(cnt[:, 0, :N_EXPERTS], tile, n_blocks)
        y = _experts(zt, blk_e, n_used, z_src, n_blocks, l, w_gate_up, b_gate_up[:, :, None, :],
                     w_down, b_down[:, :, None, :])
        h = _combine(y, y_src, rw, rpos, h1, ln2_g[l], ln2_b[l], alpha, tile)
    return h.reshape(batch, seq, d)
```

```python
import functools

import jax
import jax.numpy as jnp
import numpy as np
from jax import lax
from jax.experimental import pallas as pl
from jax.experimental.pallas import tpu as pltpu

F32 = jnp.float32
BF16 = jnp.bfloat16
I32 = jnp.int32
U32 = jnp.uint32

GRID_W = 64
HEAD_DIM = 64
N_Q_HEADS = 4
N_KV_HEADS = 2
Q_GROUP = N_Q_HEADS // N_KV_HEADS
ROPE_HALF = HEAD_DIM // 2
ROPE_THETA = 10000.0
BRANCH_W = 256
N_BRANCHES = 4
SC_WIDTH = 3
CF_WIDTH = 31
POOL_WINDOWS = (2, 4, 8, 16)
POOL_GROUP = BRANCH_W // len(POOL_WINDOWS)
POOL_TAPS = 16
POOL_FIRST = -8
N_EXPERTS = 32
TOP_K = 4
SWIGLU_LIMIT = 7.0
SWIGLU_ALPHA = 1.702
LN_EPS = 1e-5
RMS_EPS = 1e-6
QKV_COLS = N_Q_HEADS * HEAD_DIM + 2 * N_KV_HEADS * HEAD_DIM

LANES = 128
HALO = 16
ROW_GROUP = 8
MOE_TILE = 256
MOE_BLOCK = 512
PREFETCH_DEPTH = 3
SPARE_BLOCKS = 2
VMEM_LIMIT = 56 * 1024 * 1024
NEG_BIG = -1e30


def _tile_sorted_rows(t):
    return t * TOP_K + N_EXPERTS * ROW_GROUP


def _cparams(*sem):
    return pltpu.CompilerParams(dimension_semantics=sem, vmem_limit_bytes=VMEM_LIMIT)


def _tile(n, pref):
    t = min(n, pref)
    assert n % t == 0, (n, t)
    return t


def _const_spec(shape):
    nd = len(shape)
    return pl.BlockSpec(shape, lambda *_: (0,) * nd)


def _layer_norm_rows(y, g, b):
    mu = jnp.mean(y, axis=-1, keepdims=True)
    d = y - mu
    var = jnp.mean(d * d, axis=-1, keepdims=True)
    return d * lax.rsqrt(var + LN_EPS) * g + b


def _sigmoid(x):
    return 1.0 / (1.0 + jnp.exp(-x))


def _attn_tiles(seq):
    return _tile(seq, 1024), _tile(seq, 2048)


def _proj_kernel(apply_ln, *refs):
    refs = list(refs)
    x_ref = refs.pop(0)
    ln_refs = [refs.pop(0), refs.pop(0)] if apply_ln else None
    (wqkv_ref, bqkv_ref, gain_ref, scale_ref, cos_ref, sin_ref,
     wsc_ref, wcf_ref, wpl_ref, wg_ref, bsc_ref, bcf_ref, bpl_ref, bg_ref) = refs[:14]
    outs = refs[14:]
    h = x_ref[...]
    if apply_ln:
        h = _layer_norm_rows(h, ln_refs[0][...], ln_refs[1][...])
        outs.pop(0)[...] = h
    q_ref, kt_ref, v_ref, zsc_ref, zcf_ref, zpl_ref, zg_ref = outs
    hb = h.astype(BF16)

    z = jnp.dot(hb, wqkv_ref[...], preferred_element_type=F32) + bqkv_ref[...]
    t = z.shape[0]
    lane = lax.broadcasted_iota(I32, (t, LANES), 1)
    low_head = lane < HEAD_DIM
    first_half = (lane % ROPE_HALF) < (ROPE_HALF // 2)
    cos = cos_ref[...]
    sin = sin_ref[...]
    roped = []
    for c in range(3):
        x = z[:, c * LANES:(c + 1) * LANES]
        x2 = x * x
        s_all = jnp.sum(x2, axis=-1, keepdims=True)
        s_low = jnp.sum(jnp.where(low_head, x2, 0.0), axis=-1, keepdims=True)
        ms = jnp.where(low_head, s_low, s_all - s_low) * (1.0 / HEAD_DIM)
        xn = x * lax.rsqrt(ms + RMS_EPS) * gain_ref[:, c * LANES:(c + 1) * LANES]
        partner = jnp.where(first_half, pltpu.roll(xn, LANES - ROPE_HALF // 2, 1), pltpu.roll(xn, ROPE_HALF // 2, 1))
        roped.append((xn * cos + partner * sin) * scale_ref[:, c * LANES:(c + 1) * LANES])
    q_ref[:, 0:LANES] = roped[0].astype(BF16)
    q_ref[:, LANES:2 * LANES] = roped[1].astype(BF16)
    kt_ref[...] = roped[2].T.astype(BF16)
    v = z[:, 3 * LANES:4 * LANES]
    tail = jnp.where(lane == HEAD_DIM, 1.0, 0.0)
    v_ref[0] = jnp.where(low_head, v, tail).astype(BF16)
    v_ref[1] = jnp.where(low_head, pltpu.roll(v, HEAD_DIM, 1), tail).astype(BF16)

    for w_ref, b_ref, o_ref in ((wsc_ref, bsc_ref, zsc_ref), (wcf_ref, bcf_ref, zcf_ref),
                                (wpl_ref, bpl_ref, zpl_ref), (wg_ref, bg_ref, zg_ref)):
        o_ref[...] = (jnp.dot(hb, w_ref[...], preferred_element_type=F32) + b_ref[...]).astype(BF16)


def _projections(x, ln, ws, bs, gain, scale, cos, sin, batch, seq):
    n, d = x.shape
    t = _tile(seq, 256)
    tk = _attn_tiles(seq)[1]
    assert tk % t == 0
    per_seq, per_chunk, nc = seq // t, tk // t, seq // tk
    row = lambda i: (i, 0)
    rope = lambda i: (i % per_seq, 0)
    resident = lambda c: pl.BlockSpec((d, c), lambda i: (0, 0), pipeline_mode=pl.Buffered(1))
    widths = [w.shape[1] for w in ws]
    apply_ln = ln is not None
    ln_args = [ln[0].reshape(1, d), ln[1].reshape(1, d)] if apply_ln else []
    in_specs = ([pl.BlockSpec((t, d), row)] + [_const_spec((1, d))] * len(ln_args)
                + [resident(widths[0]), _const_spec((1, widths[0])), _const_spec((1, 3 * LANES)),
                   _const_spec((1, 3 * LANES)), pl.BlockSpec((t, LANES), rope), pl.BlockSpec((t, LANES), rope)]
                + [resident(c) for c in widths[1:]] + [_const_spec((1, c)) for c in widths[1:]])
    out_specs = ([pl.BlockSpec((t, d), row)] if apply_ln else []) + [
        pl.BlockSpec((t, 2 * LANES), row),
        pl.BlockSpec((None, None, LANES, t), lambda i: (i // per_seq, (i % per_seq) // per_chunk, 0, i % per_chunk)),
        pl.BlockSpec((None, None, N_KV_HEADS, t, LANES),
                     lambda i: (i // per_seq, (i % per_seq) // per_chunk, 0, i % per_chunk, 0)),
    ] + [pl.BlockSpec((t, c), row) for c in widths[1:]]
    out_shape = ([jax.ShapeDtypeStruct((n, d), F32)] if apply_ln else []) + [
        jax.ShapeDtypeStruct((n, 2 * LANES), BF16),
        jax.ShapeDtypeStruct((batch, nc, LANES, tk), BF16),
        jax.ShapeDtypeStruct((batch, nc, N_KV_HEADS, tk, LANES), BF16),
    ] + [jax.ShapeDtypeStruct((n, c), BF16) for c in widths[1:]]
    return pl.pallas_call(
        functools.partial(_proj_kernel, apply_ln),
        grid=(n // t,),
        in_specs=in_specs, out_specs=out_specs, out_shape=out_shape,
        compiler_params=_cparams("parallel"),
        name="projections",
    )(x, *ln_args, ws[0], bs[0], gain, scale, cos, sin, *ws[1:], *bs[1:])


def _attn_kernel(q_ref, kt_ref, v_ref, o_ref, m_scr, acc_scr):
    h = pl.program_id(1)
    tq = q_ref.shape[0]
    rows = Q_GROUP * tq
    n_chunks, _, tk = kt_ref.shape
    lane = lax.broadcasted_iota(I32, (tq, LANES), 1)
    own_half = (lane // HEAD_DIM) == h
    first = h == 0
    qf = q_ref[...].astype(F32)
    swapped = pltpu.roll(qf, HEAD_DIM, 1)
    q = jnp.concatenate([jnp.where(own_half, jnp.where(first, qf, swapped), 0.0),
                         jnp.where(own_half, jnp.where(first, swapped, qf), 0.0)], axis=0).astype(BF16)
    m_scr[...] = jnp.full(m_scr.shape, -jnp.inf, F32)
    acc_scr[...] = jnp.zeros(acc_scr.shape, F32)

    def body(c, carry):
        s = jnp.dot(q, kt_ref[c], preferred_element_type=F32)
        m_prev = m_scr[...]
        m_new = jnp.maximum(m_prev, jnp.max(s, axis=-1, keepdims=True))
        alpha = jnp.exp2(m_prev - m_new)
        p = jnp.exp2(s - jnp.concatenate([m_new] * (tk // LANES), axis=1)).astype(BF16)
        acc_scr[...] = alpha * acc_scr[...] + jnp.dot(p, v_ref[c], preferred_element_type=F32)
        m_scr[...] = m_new
        return carry

    lax.fori_loop(0, n_chunks, body, 0, unroll=2 if n_chunks % 2 == 0 else 1)
    acc = acc_scr[...]
    res = acc / acc[:, HEAD_DIM:HEAD_DIM + 1]
    o_ref[...] = jnp.where(lane < HEAD_DIM, res[:tq], pltpu.roll(res[tq:], HEAD_DIM, 1)).astype(BF16)


def _attention(q, kt, v, batch, seq):
    tq, tk = _attn_tiles(seq)
    nc = seq // tk
    per_seq = seq // tq
    rows = Q_GROUP * tq
    return pl.pallas_call(
        _attn_kernel,
        grid=(batch, N_KV_HEADS, per_seq),
        in_specs=[pl.BlockSpec((tq, LANES), lambda b, h, i: (b * per_seq + i, h)),
                  pl.BlockSpec((None, nc, LANES, tk), lambda b, h, i: (b, 0, 0, 0)),
                  pl.BlockSpec((None, nc, None, tk, LANES), lambda b, h, i: (b, 0, h, 0, 0))],
        out_specs=pl.BlockSpec((None, tq, LANES), lambda b, h, i: (b, i, h)),
        out_shape=jax.ShapeDtypeStruct((batch, seq, N_KV_HEADS * LANES), BF16),
        scratch_shapes=[pltpu.VMEM((rows, LANES), F32), pltpu.VMEM((rows, LANES), F32)],
        compiler_params=_cparams("parallel", "parallel", "arbitrary"),
        name="attention",
    )(q, kt, v).reshape(batch * seq, N_KV_HEADS * LANES)


def _dwconv(xs_ref, w_ref, n_taps, first_row, rows, shifted_ref=None):
    if shifted_ref is None:
        taps = [xs_ref[pl.ds(first_row + j, rows), :] for j in range(n_taps)]
    else:
        span = shifted_ref.shape[1]
        assert first_row + n_taps - 1 + rows <= span + ROW_GROUP - 1
        shifts = sorted({(first_row + j) % ROW_GROUP for j in range(n_taps)} - {0})
        for s in shifts:
            shifted_ref[s] = xs_ref[pl.ds(s, span), :]
        taps = []
        for j in range(n_taps):
            s, base = (first_row + j) % ROW_GROUP, (first_row + j) // ROW_GROUP * ROW_GROUP
            src = xs_ref if s == 0 else shifted_ref.at[s]
            taps.append(src[pl.ds(base, rows), :])
    acc = None
    for j, tap in enumerate(taps):
        term = tap * w_ref[j:j + 1, :]
        acc = term if acc is None else acc + term
    return acc


def _mixer_kernel(seq, alpha, *refs):
    zt_ref = refs[-5]
    n_tiles = pl.num_programs(0) - SPARE_BLOCKS

    @pl.when(pl.program_id(0) >= n_tiles)
    def _():
        zt_ref[...] = jnp.zeros(zt_ref.shape, zt_ref.dtype)

    @pl.when(pl.program_id(0) < n_tiles)
    def _():
        _mixer_tile(seq, alpha, *refs)


def _mixer_tile(seq, alpha,
                o_ref, scp_ref, scc_ref, scn_ref, cfp_ref, cfc_ref, cfn_ref, plp_ref, plc_ref, pln_ref,
                zg_ref, h_ref, scw_ref, cfw_ref, cfb_ref, cfg_ref, cfbeta_ref, plmask_ref, pllo_ref, plhi_ref,
                plw_ref, pls_ref, wbr_ref, wout_ref, ln1g_ref, ln1b_ref, wr_ref, br_ref,
                h1_ref, rw_ref, rpos_ref, cnt_ref, zt_ref,
                xsc, xcf, xpl, xsh):
    t = h_ref.shape[0]
    w = BRANCH_W
    i = pl.program_id(0)
    per_seq = seq // t
    j = i % per_seq
    has_prev = (j > 0).astype(F32)
    has_next = (j < per_seq - 1).astype(F32)

    def sc_v(z):
        z = z.astype(F32)
        return z[:, w:2 * w] * z[:, 2 * w:3 * w]
    xsc[0:HALO, :] = sc_v(scp_ref[...]) * has_prev
    xsc[HALO:HALO + t, :] = sc_v(scc_ref[...])
    xsc[HALO + t:2 * HALO + t, :] = sc_v(scn_ref[...]) * has_next
    y_sc = scc_ref[:, 0:w].astype(F32) * _dwconv(xsc, scw_ref, SC_WIDTH, HALO - SC_WIDTH // 2, t)

    def cf_v(z):
        z = z.astype(F32)
        return z[:, 0:w] * _sigmoid(z[:, w:2 * w])
    xcf[0:HALO, :] = cf_v(cfp_ref[...]) * has_prev
    xcf[HALO:HALO + t, :] = cf_v(cfc_ref[...])
    xcf[HALO + t:2 * HALO + t, :] = cf_v(cfn_ref[...]) * has_next
    y_cf = _dwconv(xcf, cfw_ref, CF_WIDTH, HALO - CF_WIDTH // 2, t, xsh) + cfb_ref[...]
    y_cf = _layer_norm_rows(y_cf, cfg_ref[...], cfbeta_ref[...])
    y_cf = y_cf * _sigmoid(y_cf)

    u = plc_ref[...].astype(F32)
    xpl[0:HALO, :] = plp_ref[...].astype(F32) * has_prev
    xpl[HALO:HALO + t, :] = u
    xpl[HALO + t:2 * HALO + t, :] = pln_ref[...].astype(F32) * has_next
    win_sum = _dwconv(xpl, plmask_ref, POOL_TAPS, HALO + POOL_FIRST, t, xsh)
    pos = (j * t + lax.broadcasted_iota(I32, (t, w), 0)).astype(F32)
    n_win = jnp.minimum(pos + plhi_ref[...], float(seq - 1)) - jnp.maximum(pos - pllo_ref[...], 0.0) + 1.0
    y_pl = win_sum / n_win - u
    y_pl = jnp.dot(y_pl.astype(BF16), plw_ref[...], preferred_element_type=F32) * pls_ref[...]

    merged = None
    for g, br in enumerate((o_ref[...], y_sc.astype(BF16), y_cf.astype(BF16), y_pl.astype(BF16))):
        proj = jnp.dot(br, wbr_ref[g], preferred_element_type=F32)
        gate = 1.0 / (1.0 + jnp.exp2(zg_ref[:, g * proj.shape[1]:(g + 1) * proj.shape[1]].astype(F32)))
        merged = gate * proj if merged is None else merged + gate * proj
    mix = jnp.dot(merged.astype(BF16), wout_ref[...], preferred_element_type=F32)
    h1 = _layer_norm_rows(alpha * h_ref[...] + mix, ln1g_ref[...], ln1b_ref[...])
    h1_ref[...] = h1

    logits = jnp.dot(h1.astype(BF16), wr_ref[...], preferred_element_type=F32) + br_ref[...]
    lane = lax.broadcasted_iota(I32, (t, LANES), 1)
    vals = logits
    tops, sels = [], []
    for _ in range(TOP_K):
        m = jnp.max(vals, axis=-1, keepdims=True)
        first = jnp.min(jnp.where(vals == m, lane, LANES), axis=-1, keepdims=True)
        sel = lane == first
        vals = jnp.where(sel, -jnp.inf, vals)
        tops.append(m)
        sels.append(sel)
    exps = [jnp.exp(m - tops[0]) for m in tops]
    denom = exps[0] + exps[1] + exps[2] + exps[3]

    onehot = jnp.where(sels[0] | sels[1] | sels[2] | sels[3], 1.0, 0.0)
    r_id = lax.broadcasted_iota(I32, (t, t), 0)
    c_id = lax.broadcasted_iota(I32, (t, t), 1)
    lower = jnp.where(c_id < r_id, 1.0, 0.0).astype(BF16)
    rank_e = jnp.dot(lower, onehot.astype(BF16), preferred_element_type=F32)
    cnt = jnp.sum(onehot, axis=0, keepdims=True)
    groups = jnp.floor((cnt + (ROW_GROUP - 1.0)) * (1.0 / ROW_GROUP))
    e_r = lax.broadcasted_iota(I32, (LANES, LANES), 0)
    e_c = lax.broadcasted_iota(I32, (LANES, LANES), 1)
    before = jnp.where(e_r < e_c, 1.0, 0.0).astype(BF16)
    strip_start = jnp.dot(jnp.broadcast_to(groups, (ROW_GROUP, LANES)).astype(BF16), before,
                          preferred_element_type=F32)[0:1, :]
    row_e = rank_e + ROW_GROUP * strip_start
    cnt_ref[...] = jnp.broadcast_to(cnt, cnt_ref.shape).astype(I32)

    zr = zt_ref.shape[0]
    z_row = lax.broadcasted_iota(I32, (t, zr), 1)
    rw = jnp.zeros((t, LANES), F32)
    rpos = jnp.zeros((t, LANES), I32)
    scatter = jnp.zeros((t, zr), F32)
    for k in range(TOP_K):
        pos_k = jnp.sum(jnp.where(sels[k], row_e, 0.0), axis=-1, keepdims=True).astype(I32)
        scatter = jnp.where(z_row == pos_k, 1.0, scatter)
        rw = jnp.where(lane == k, exps[k] / denom, rw)
        rpos = jnp.where(lane == k, pos_k, rpos)
    rw_ref[...] = rw
    rpos_ref[...] = rpos
    zt_ref[...] = _pack_bf16_pairs(lax.dot_general(scatter.astype(BF16), h1.astype(BF16), (((0,), (0,)), ((), ())),
                                                   preferred_element_type=F32))


def _mixer(o, zsc, zcf, zpl, zg, h, p, seq, alpha):
    n, d = h.shape
    t = _tile(seq, MOE_TILE)
    hb = t // HALO
    n_halo = n // HALO
    n_tiles = n // t
    tile = lambda s: jnp.minimum(s, n_tiles - 1)
    row = lambda s: (tile(s), 0)
    prev = lambda s: (jnp.maximum(tile(s) * hb - 1, 0), 0)
    nxt = lambda s: (jnp.minimum((tile(s) + 1) * hb, n_halo - 1), 0)

    def halo_specs(c):
        return [pl.BlockSpec((HALO, c), prev), pl.BlockSpec((t, c), row), pl.BlockSpec((HALO, c), nxt)]

    consts = [p["sc_w"], p["cf_w"], p["cf_b"], p["cf_g"], p["cf_beta"], p["pl_mask"], p["pl_lo"], p["pl_hi"],
              p["pl_w"], p["pl_s"], p["w_br"], p["w_out"], p["ln1_g"], p["ln1_b"], p["w_r"], p["b_r"]]
    lane_out = lambda dt: jax.ShapeDtypeStruct((n, LANES), dt)
    zr = _tile_sorted_rows(t)
    return pl.pallas_call(
        functools.partial(_mixer_kernel, seq, alpha),
        grid=(n_tiles + SPARE_BLOCKS,),
        in_specs=[pl.BlockSpec((t, BRANCH_W), row)] + halo_specs(3 * BRANCH_W) + halo_specs(2 * BRANCH_W)
        + halo_specs(BRANCH_W) + [pl.BlockSpec((t, N_BRANCHES * d), row), pl.BlockSpec((t, d), row)]
        + [_const_spec(c.shape) for c in consts],
        out_specs=[pl.BlockSpec((t, d), row), pl.BlockSpec((t, LANES), row),
                   pl.BlockSpec((t, LANES), row), pl.BlockSpec((None, ROW_GROUP, LANES), lambda s: (tile(s), 0, 0)),
                   pl.BlockSpec((zr, d // 2), lambda s: (s, 0))],
        out_shape=[jax.ShapeDtypeStruct((n, d), F32), lane_out(F32), lane_out(I32),
                   jax.ShapeDtypeStruct((n_tiles, ROW_GROUP, LANES), I32),
                   jax.ShapeDtypeStruct(((n_tiles + SPARE_BLOCKS) * zr, d // 2), U32)],
        scratch_shapes=[pltpu.VMEM((t + 2 * HALO, BRANCH_W), F32)] * 3
        + [pltpu.VMEM((ROW_GROUP, t + 2 * HALO - ROW_GROUP, BRANCH_W), F32)],
        compiler_params=_cparams("arbitrary"),
        name="mixer",
    )(o, zsc, zsc, zsc, zcf, zcf, zcf, zpl, zpl, zpl, zg, h, *consts)


def _group_copy(src_hbm, group, dst, dst_group, sem):
    return pltpu.make_async_copy(src_hbm.at[pl.ds(pl.multiple_of(group * ROW_GROUP, ROW_GROUP), ROW_GROUP), :],
                                 dst.at[pl.ds(dst_group * ROW_GROUP, ROW_GROUP), :], sem)


def _pack_bf16_pairs(y):
    c = y.shape[1] // 2
    bits = lax.bitcast_convert_type(y.astype(BF16).astype(F32), U32)
    return (bits[:, :c] >> 16) | bits[:, c:]


def _unpack_bf16_pairs(words):
    low = lax.bitcast_convert_type(words << 16, F32)
    high = lax.bitcast_convert_type(words & jnp.uint32(0xFFFF0000), F32)
    return low.astype(BF16), high.astype(BF16)


def _cast_rows(src_ref, dst_ref, chunk):
    def body(r, c):
        rows = pl.ds(pl.multiple_of(r * chunk, chunk), chunk)
        dst_ref[rows, :] = src_ref[rows, :].astype(dst_ref.dtype)
        return c
    lax.fori_loop(0, src_ref.shape[0] // chunk, body, 0)


def _expert_kernel(blk_e_ref, n_used_ref, src_ref, dst_ref, zt_hbm, wgu32_ref, bgu_ref, wd32_ref, bd_ref, y_hbm,
                   xbuf, ybuf, wgu_ref, wd_ref, sem, ysem):
    b = pl.program_id(0)
    n_used = n_used_ref[0]
    groups = MOE_BLOCK // ROW_GROUP

    @pl.when((b < n_used) & ((b == 0) | (blk_e_ref[b] != blk_e_ref[jnp.maximum(b - 1, 0)])))
    def _():
        _cast_rows(wgu32_ref, wgu_ref, LANES)
        _cast_rows(wd32_ref, wd_ref, LANES)

    def gather(block, slot):
        used = block < n_used
        block = jnp.minimum(block, n_used - 1)
        idle = zt_hbm.shape[0] // ROW_GROUP - groups
        for g in range(groups):
            group = jnp.where(used, src_ref[block * groups + g], idle + g)
            _group_copy(zt_hbm, group, xbuf.at[slot], g, sem.at[slot]).start()

    def wait(slot):
        pltpu.make_async_copy(xbuf.at[slot], xbuf.at[slot], sem.at[slot]).wait()

    def wait_results(slot):
        pltpu.make_async_copy(ybuf.at[slot], ybuf.at[slot], ysem.at[slot]).wait()

    @pl.when(b == 0)
    def _():
        gather(0, 0)
        gather(1, 1)

    @pl.when((b >= 2) & (b < n_used))
    def _():
        wait_results(b % 2)

    @pl.when(b < n_used)
    def _():
        wait(b % PREFETCH_DEPTH)
        f = wd_ref.shape[0]
        x = jnp.concatenate(_unpack_bf16_pairs(xbuf[b % PREFETCH_DEPTH]), axis=1)
        hgu = jnp.dot(x, wgu_ref[...], preferred_element_type=F32) + bgu_ref[...]
        glu = jnp.minimum(hgu[:, :f], SWIGLU_LIMIT)
        lin = jnp.clip(hgu[:, f:], -SWIGLU_LIMIT, SWIGLU_LIMIT)
        act = glu * _sigmoid(SWIGLU_ALPHA * glu) * (lin + 1.0)
        y = jnp.dot(act.astype(BF16), wd_ref[...], preferred_element_type=F32) + bd_ref[...]
        ybuf[b % 2] = _pack_bf16_pairs(y)
        for g in range(groups):
            dst = pl.multiple_of(dst_ref[b * groups + g] * ROW_GROUP, ROW_GROUP)
            pltpu.make_async_copy(ybuf.at[b % 2, pl.ds(g * ROW_GROUP, ROW_GROUP), :],
                                  y_hbm.at[pl.ds(dst, ROW_GROUP), :], ysem.at[b % 2]).start()
        gather(b + 2, (b + 2) % PREFETCH_DEPTH)

    @pl.when(b == n_used - 1)
    def _():
        wait((b + 1) % PREFETCH_DEPTH)
        wait((b + 2) % PREFETCH_DEPTH)
        wait_results(b % 2)

        @pl.when(b >= 1)
        def _():
            wait_results((b + 1) % 2)


def _experts(zt, blk_e, n_used, src_groups, dst_groups, n_blocks, layer, wgu, bgu, wd, bd):
    d = 2 * zt.shape[1]
    f2 = wgu.shape[3]
    expert = lambda b, be, nu, sg, dg: (layer, be[b], 0, 0)
    return pl.pallas_call(
        _expert_kernel,
        grid_spec=pltpu.PrefetchScalarGridSpec(
            num_scalar_prefetch=4,
            grid=(n_blocks,),
            in_specs=[pl.BlockSpec(memory_space=pl.ANY),
                      pl.BlockSpec((None, None, d, f2), expert), pl.BlockSpec((None, None, 1, f2), expert),
                      pl.BlockSpec((None, None, f2 // 2, d), expert), pl.BlockSpec((None, None, 1, d), expert)],
            out_specs=pl.BlockSpec(memory_space=pl.ANY),
            scratch_shapes=[pltpu.VMEM((PREFETCH_DEPTH, MOE_BLOCK, d // 2), U32), pltpu.VMEM((2, MOE_BLOCK, d // 2), U32),
                            pltpu.VMEM((d, f2), BF16), pltpu.VMEM((f2 // 2, d), BF16),
                            pltpu.SemaphoreType.DMA((PREFETCH_DEPTH,)), pltpu.SemaphoreType.DMA((2,))],
        ),
        out_shape=jax.ShapeDtypeStruct(zt.shape, U32),
        input_output_aliases={4: 0},
        compiler_params=_cparams("arbitrary"),
        name="moe_experts",
    )(blk_e, n_used, src_groups, dst_groups, zt, wgu, bgu, wd, bd)


def _combine_kernel(alpha, y_ref, rw_ref, rpos_ref, h1_ref, g_ref, b_ref, h_ref):
    zr = y_ref.shape[0]
    t = h1_ref.shape[0]
    lane = lax.broadcasted_iota(I32, (t, LANES), 1)
    z_row = lax.broadcasted_iota(I32, (t, zr), 1)
    rw = rw_ref[...]
    rpos = rpos_ref[...]
    weights = jnp.zeros((t, zr), F32)
    for k in range(TOP_K):
        w_k = jnp.sum(jnp.where(lane == k, rw, 0.0), axis=-1, keepdims=True)
        pos_k = jnp.sum(jnp.where(lane == k, rpos, 0), axis=-1, keepdims=True)
        weights = jnp.where(z_row == pos_k, w_k, weights)
    wb = weights.astype(BF16)
    y_low, y_high = _unpack_bf16_pairs(y_ref[...])
    ffn = jnp.concatenate([jnp.dot(wb, y_low, preferred_element_type=F32),
                           jnp.dot(wb, y_high, preferred_element_type=F32)], axis=1)
    h_ref[...] = _layer_norm_rows(alpha * h1_ref[...] + ffn, g_ref[...], b_ref[...])


def _combine(y, rw, rpos, h1, g, b, alpha, tile):
    n, d = h1.shape
    zr = _tile_sorted_rows(tile)
    row = lambda i: (i, 0)
    return pl.pallas_call(
        functools.partial(_combine_kernel, alpha),
        grid=(n // tile,),
        in_specs=[pl.BlockSpec((zr, d // 2), row), pl.BlockSpec((tile, LANES), row),
                  pl.BlockSpec((tile, LANES), row), pl.BlockSpec((tile, d), row),
                  _const_spec((1, d)), _const_spec((1, d))],
        out_specs=pl.BlockSpec((tile, d), row),
        out_shape=jax.ShapeDtypeStruct((n, d), F32),
        compiler_params=_cparams("parallel"),
        name="moe_combine",
    )(y, rw, rpos, h1, g.reshape(1, d), b.reshape(1, d))


def _rope_tables(seq):
    rows = seq // GRID_W
    row = jnp.repeat(jnp.arange(rows, dtype=F32), GRID_W)
    col = jnp.tile(jnp.arange(GRID_W, dtype=F32), rows)
    inv = ROPE_THETA ** (-jnp.arange(0, ROPE_HALF, 2, dtype=F32) / ROPE_HALF)
    ang_r = row[:, None] * inv
    ang_c = col[:, None] * inv
    cr, sr, cc, sc = jnp.cos(ang_r), jnp.sin(ang_r), jnp.cos(ang_c), jnp.sin(ang_c)
    cos = jnp.concatenate([cr, cr, cc, cc], axis=-1)
    sin = jnp.concatenate([-sr, sr, -sc, sc], axis=-1)
    return jnp.tile(cos, (1, 2)), jnp.tile(sin, (1, 2))


def _pool_constants():
    offs = np.arange(POOL_TAPS) + POOL_FIRST
    mask = np.zeros((POOL_TAPS, BRANCH_W), np.float32)
    lo = np.zeros((1, BRANCH_W), np.float32)
    hi = np.zeros((1, BRANCH_W), np.float32)
    for g, win in enumerate(POOL_WINDOWS):
        sl = slice(g * POOL_GROUP, (g + 1) * POOL_GROUP)
        mask[(offs >= -(win // 2)) & (offs <= win - 1 - win // 2), sl] = 1.0
        lo[0, sl] = win // 2
        hi[0, sl] = win - 1 - win // 2
    return jnp.asarray(mask), jnp.asarray(lo), jnp.asarray(hi)


def _block_diag(blocks):
    g, a, b = blocks.shape
    out = jnp.zeros((g * a, g * b), blocks.dtype)
    for i in range(g):
        out = out.at[i * a:(i + 1) * a, i * b:(i + 1) * b].set(blocks[i])
    return out


def _moe_tables(tile_counts, tile, n_blocks):
    nt = tile_counts.shape[0]
    bg = MOE_BLOCK // ROW_GROUP
    gz = _tile_sorted_rows(tile) // ROW_GROUP
    assert 2 * bg <= gz
    ng = (tile_counts + ROW_GROUP - 1) // ROW_GROUP
    strip_end = jnp.cumsum(ng, axis=1)
    strip_start = strip_end - ng
    run_end = jnp.cumsum(ng, axis=0)
    run_start = run_end - ng
    total = run_end[-1]
    blocks_e = (total + bg - 1) // bg
    blk_end = jnp.cumsum(blocks_e)
    blk_start = blk_end - blocks_e
    n_used = blk_end[-1:].astype(I32)
    b_ids = jnp.arange(n_blocks, dtype=I32)
    blk_e = jnp.minimum(jnp.sum(blk_end[None, :] <= b_ids[:, None], axis=1), N_EXPERTS - 1).astype(I32)

    experts = jnp.arange(N_EXPERTS, dtype=I32)
    is_e = blk_e[:, None] == experts[None, :]
    pick = lambda tab: jnp.sum(jnp.where(is_e, tab[None, :], 0), axis=1)
    columns = lambda tab: jnp.dot(is_e.astype(F32), tab.T.astype(F32),
                                  precision=lax.Precision.HIGHEST).astype(I32)
    q = (b_ids - pick(blk_start))[:, None] * bg + jnp.arange(bg, dtype=I32)[None, :]
    starts, ends = columns(run_start), columns(run_end)
    offset = columns(jnp.arange(nt, dtype=I32)[:, None] * gz + strip_start - run_start)
    in_strip = (starts[:, None, :] <= q[:, :, None]) & (q[:, :, None] < ends[:, None, :])
    valid = q < pick(total)[:, None]
    group = jnp.sum(jnp.where(in_strip, offset[:, None, :], 0), axis=2) + q
    z_src = jnp.where(valid, group, group[:, :1])
    spare = nt * gz + (b_ids % 2)[:, None] * bg + jnp.arange(bg, dtype=I32)[None, :]
    z_dst = jnp.where(valid, group, spare)
    return blk_e, n_used, z_src.reshape(-1).astype(I32), z_dst.reshape(-1).astype(I32)


def kernel(x, ln_in_g, ln_in_b, w_in, b_in, q_norm_g, k_norm_g, sc_conv_w, cf_conv_w, cf_conv_b, cf_ln_g, cf_ln_b, pool_w, pool_scale, w_branch, w_out, ln1_g, ln1_b, w_router, b_router, w_gate_up, b_gate_up, w_down, b_down, ln2_g, ln2_b):
    batch, seq, d = x.shape
    depth = w_in.shape[0]
    n = batch * seq
    alpha = float((2.0 * depth) ** 0.25)
    tile = _tile(seq, MOE_TILE)
    n_blocks = pl.cdiv((n // tile) * _tile_sorted_rows(tile), MOE_BLOCK) + N_EXPERTS

    cos, sin = _rope_tables(seq)
    pl_mask, pl_lo, pl_hi = _pool_constants()
    q_scale = jnp.concatenate([jnp.full((1, 2 * LANES), HEAD_DIM ** -0.5 * np.log2(np.e), F32),
                               jnp.ones((1, LANES), F32)], axis=-1)
    splits = np.cumsum([QKV_COLS, 3 * BRANCH_W, 2 * BRANCH_W, BRANCH_W])

    h = x.reshape(n, d)
    for l in range(depth):
        col_scale = jnp.concatenate([jnp.ones((splits[3],), F32), jnp.full((w_in.shape[2] - splits[3],), -np.log2(np.e), F32)])
        wl = (w_in[l] * col_scale).astype(BF16)
        bl = (b_in[l] * col_scale).reshape(1, -1)
        w_parts = [wl[:, :splits[0]]] + [wl[:, splits[i]:splits[i + 1]] for i in range(3)] + [wl[:, splits[3]:]]
        b_parts = [bl[:, :splits[0]]] + [bl[:, splits[i]:splits[i + 1]] for i in range(3)] + [bl[:, splits[3]:]]
        gain = jnp.concatenate([jnp.tile(q_norm_g[l], N_Q_HEADS), jnp.tile(k_norm_g[l], N_KV_HEADS)]).reshape(1, -1)

        outs = _projections(h, (ln_in_g, ln_in_b) if l == 0 else None, w_parts, b_parts, gain, q_scale, cos, sin,
                            batch, seq)
        if l == 0:
            h, *outs = outs
        q, kt, v, zsc, zcf, zpl, zg = outs
        o = _attention(q, kt, v, batch, seq)

        r_pad = LANES - N_EXPERTS
        params = dict(
            sc_w=sc_conv_w[l], cf_w=cf_conv_w[l], cf_b=cf_conv_b[l].reshape(1, -1), cf_g=cf_ln_g[l].reshape(1, -1),
            cf_beta=cf_ln_b[l].reshape(1, -1), pl_mask=pl_mask, pl_lo=pl_lo, pl_hi=pl_hi,
            pl_w=_block_diag(pool_w[l]).astype(BF16), pl_s=pool_scale[l].reshape(1, -1),
            w_br=w_branch[l].astype(BF16), w_out=w_out[l].astype(BF16),
            ln1_g=ln1_g[l].reshape(1, -1), ln1_b=ln1_b[l].reshape(1, -1),
            w_r=jnp.pad(w_router[l], ((0, 0), (0, r_pad))).astype(BF16),
            b_r=jnp.pad(b_router[l].astype(F32), (0, r_pad), constant_values=NEG_BIG).reshape(1, -1),
        )
        h1, rw, rpos, cnt, zt = _mixer(o, zsc, zcf, zpl, zg, h, params, seq, alpha)

        blk_e, n_used, z_src, z_dst = _moe_tables(cnt[:, 0, :N_EXPERTS], tile, n_blocks)
        y = _experts(zt, blk_e, n_used, z_src, z_dst, n_blocks, l, w_gate_up, b_gate_up[:, :, None, :],
                     w_down, b_down[:, :, None, :])
        h = _combine(y, rw, rpos, h1, ln2_g[l], ln2_b[l], alpha, tile)
    return h.reshape(batch, seq, d)
```
